```python
import math
import jax, jax.numpy as jnp
from jax import lax
import numpy as np

D_MODEL = 1024
BATCH = 8
SEQ = 2048
DEPTH = 1
DEC_BATCH = 128
DEC_SEQ = 8
PAST_LEN = 8192
PAGE_SIZE = 128

N_HEADS = 16
HEAD_DIM = D_MODEL // N_HEADS
N_KV_HEADS = 4
GQA_GROUP = N_HEADS // N_KV_HEADS
ROT_DIM = HEAD_DIM // 4
ROPE_THETA = 500000.0
WINDOW = 128
BLOCK = WINDOW
CONV_CH = D_MODEL
CONV_WIDTH = 31
D_FF = ((8 * D_MODEL // 3 + 127) // 128) * 128
EPS = 1e-6
NEG_INF = -1e30
ATTN_W = N_HEADS * HEAD_DIM
KV_W = N_KV_HEADS * HEAD_DIM
IN_COLS = 2 * CONV_CH + ATTN_W + 2 * KV_W + 2 * D_MODEL

kernel_name = 'hybrid_conformer_conv_swa_sink_decode_step'


def rmsnorm(x, g):
    xf = x.astype(jnp.float32)
    r = lax.rsqrt(jnp.mean(xf * xf, axis=-1, keepdims=True) + EPS)
    return (xf * r * g.astype(jnp.float32)).astype(x.dtype)


def layernorm(x, g, b):
    xf = x.astype(jnp.float32)
    mu = jnp.mean(xf, axis=-1, keepdims=True)
    var = jnp.mean(jnp.square(xf - mu), axis=-1, keepdims=True)
    return ((xf - mu) * lax.rsqrt(var + EPS) * g.astype(jnp.float32) + b.astype(jnp.float32)).astype(x.dtype)


def swiglu(x, w_up, w_down):
    g, u = jnp.split(x @ w_up, 2, axis=-1)
    return (jax.nn.silu(g) * u) @ w_down


def rope(x, pos):
    inv = jnp.exp(-math.log(ROPE_THETA) * jnp.arange(0, ROT_DIM, 2, dtype=jnp.float32) / ROT_DIM)
    ang = pos.astype(jnp.float32)[:, None] * inv[None, :]
    cos = jnp.cos(ang)[:, None, :]
    sin = jnp.sin(ang)[:, None, :]
    xr = x[..., :ROT_DIM].astype(jnp.float32)
    x1, x2 = xr[..., :ROT_DIM // 2], xr[..., ROT_DIM // 2:]
    rot = jnp.concatenate([x1 * cos - x2 * sin, x2 * cos + x1 * sin], axis=-1).astype(x.dtype)
    return jnp.concatenate([rot, x[..., ROT_DIM:]], axis=-1)


def sink_attention(q, k, v, mask, sinks):
    s = jnp.einsum('bnqkgd,bnskd->bnkgqs', q, k).astype(jnp.float32) * (HEAD_DIM ** -0.5)
    s = jnp.where(mask[None, :, None, None], s, NEG_INF)
    sink = jnp.broadcast_to(sinks.astype(jnp.float32).reshape(N_KV_HEADS, GQA_GROUP, 1, 1), s.shape[:-1] + (1,))
    prob = jax.nn.softmax(jnp.concatenate([s, sink], axis=-1), axis=-1)[..., :-1]
    return jnp.einsum('bnkgqs,bnskd->bnqkgd', prob.astype(v.dtype), v)


def window_mask(q_pos, k_pos):
    return (k_pos <= q_pos) & (q_pos - k_pos < WINDOW) & (k_pos >= 0)


def prompt_attend(q, k, v, sinks):
    B, T = q.shape[:2]
    nb = T // BLOCK
    qb = q.reshape(B, nb, BLOCK, N_KV_HEADS, GQA_GROUP, HEAD_DIM)
    kb = k.reshape(B, nb, BLOCK, N_KV_HEADS, HEAD_DIM)
    vb = v.reshape(B, nb, BLOCK, N_KV_HEADS, HEAD_DIM)
    prev = lambda t: jnp.concatenate([jnp.zeros_like(t[:, :1]), t[:, :-1]], axis=1)
    kk = jnp.concatenate([prev(kb), kb], axis=2)
    vv = jnp.concatenate([prev(vb), vb], axis=2)
    blk = jnp.arange(nb)[:, None, None]
    q_pos = blk * BLOCK + jnp.arange(BLOCK)[None, :, None]
    k_pos = (blk - 1) * BLOCK + jnp.arange(2 * BLOCK)[None, None, :]
    o = sink_attention(qb, kk, vv, window_mask(q_pos, k_pos), sinks).reshape(B, T, ATTN_W)
    return o, k[:, T - WINDOW:], v[:, T - WINDOW:]


def sample_attend(q, k, v, sinks, k_buf, v_buf):
    Bd, S = q.shape[:2]
    L = k_buf.shape[1]
    kk = jnp.concatenate([k_buf.astype(k.dtype), k], axis=1)
    vv = jnp.concatenate([v_buf.astype(v.dtype), v], axis=1)
    q_pos = (PAST_LEN + jnp.arange(S))[None, :, None]
    k_pos = (PAST_LEN - L + jnp.arange(L + S))[None, None, :]
    qb = q.reshape(Bd, 1, S, N_KV_HEADS, GQA_GROUP, HEAD_DIM)
    o = sink_attention(qb, kk[:, None], vv[:, None], window_mask(q_pos, k_pos), sinks).reshape(Bd, S, ATTN_W)
    return o, kk[:, L + S - L:], vv[:, L + S - L:]


def causal_depthwise_conv(full, w, b):
    y = lax.conv_general_dilated(full, w[:, None, :].astype(full.dtype), window_strides=(1,), padding='VALID',
                                 dimension_numbers=('NWC', 'WIO', 'NWC'), feature_group_count=CONV_CH)
    return y + b


def decoder_layer(x, pos, conv_prefix, attend, p):
    x = x + 0.5 * rmsnorm(swiglu(rmsnorm(x, p['ffn1_pre_g']), p['ffn1_w_up'], p['ffn1_w_down']), p['ffn1_post_g'])
    h = rmsnorm(x, p['mix_pre_g'])
    z = h @ p['w_in']
    o1 = 2 * CONV_CH
    o2 = o1 + ATTN_W
    o3 = o2 + KV_W
    o4 = o3 + KV_W
    B, T = x.shape[:2]
    u = z[..., :CONV_CH] * jax.nn.sigmoid(z[..., CONV_CH:o1])
    full = jnp.concatenate([conv_prefix.astype(u.dtype), u], axis=1)
    c = causal_depthwise_conv(full, p['conv_dw_w'], p['conv_dw_b'])
    c = jax.nn.silu(layernorm(c, p['conv_ln_g'], p['conv_ln_b']))
    conv_out = c @ p['w_conv_out']
    conv_state = full[:, full.shape[1] - (CONV_WIDTH - 1):]
    q = rope(z[..., o1:o2].reshape(B, T, N_HEADS, HEAD_DIM), pos)
    k = rope(z[..., o2:o3].reshape(B, T, N_KV_HEADS, HEAD_DIM), pos)
    v = z[..., o3:o4].reshape(B, T, N_KV_HEADS, HEAD_DIM)
    o, k_state, v_state = attend(q, k, v, p['attn_sinks'])
    attn_out = o @ p['w_attn_out']
    g_conv, g_attn = jnp.split(jax.nn.sigmoid(z[..., o4:]), 2, axis=-1)
    y = (g_conv * conv_out + g_attn * attn_out) @ p['w_out']
    x = x + rmsnorm(y, p['mix_post_g'])
    x = x + 0.5 * rmsnorm(swiglu(rmsnorm(x, p['ffn2_pre_g']), p['ffn2_w_up'], p['ffn2_w_down']), p['ffn2_post_g'])
    return x, conv_state, k_state, v_state


def setup_inputs(seed: int = 0) -> dict:
    key = jax.random.key(seed)
    ks = iter(jax.random.split(key, 32))
    f32 = jnp.float32
    nrm = lambda shape, scale: jax.random.normal(next(ks), shape, f32) * scale
    gain = lambda n: 1.0 + 0.05 * jax.random.normal(next(ks), (DEPTH, n), f32)
    win_buf = min(WINDOW, PAST_LEN)
    d = {}
    d['x_prompt'] = nrm((BATCH, SEQ, D_MODEL), 1.0)
    d['x_sample'] = nrm((DEC_BATCH, DEC_SEQ, D_MODEL), 1.0)
    d['state_conv'] = nrm((DEPTH, DEC_BATCH, CONV_WIDTH - 1, CONV_CH), 0.5)
    d['cache_k_win'] = nrm((DEPTH, DEC_BATCH, win_buf, N_KV_HEADS, HEAD_DIM), 1.0)
    d['cache_v_win'] = nrm((DEPTH, DEC_BATCH, win_buf, N_KV_HEADS, HEAD_DIM), 1.0)
    d['ffn1_pre_g'] = gain(D_MODEL)
    d['ffn1_w_up'] = nrm((DEPTH, D_MODEL, 2 * D_FF), D_MODEL ** -0.5)
    d['ffn1_w_down'] = nrm((DEPTH, D_FF, D_MODEL), D_FF ** -0.5)
    d['ffn1_post_g'] = gain(D_MODEL)
    d['mix_pre_g'] = gain(D_MODEL)
    d['w_in'] = nrm((DEPTH, D_MODEL, IN_COLS), D_MODEL ** -0.5)
    d['conv_dw_w'] = nrm((DEPTH, CONV_WIDTH, CONV_CH), CONV_WIDTH ** -0.5)
    d['conv_dw_b'] = nrm((DEPTH, CONV_CH), 0.02)
    d['conv_ln_g'] = gain(CONV_CH)
    d['conv_ln_b'] = nrm((DEPTH, CONV_CH), 0.02)
    d['w_conv_out'] = nrm((DEPTH, CONV_CH, D_MODEL), CONV_CH ** -0.5)
    d['attn_sinks'] = nrm((DEPTH, N_HEADS), 1.0)
    d['w_attn_out'] = nrm((DEPTH, ATTN_W, D_MODEL), ATTN_W ** -0.5)
    d['w_out'] = nrm((DEPTH, D_MODEL, D_MODEL), D_MODEL ** -0.5)
    d['mix_post_g'] = gain(D_MODEL)
    d['ffn2_pre_g'] = gain(D_MODEL)
    d['ffn2_w_up'] = nrm((DEPTH, D_MODEL, 2 * D_FF), D_MODEL ** -0.5)
    d['ffn2_w_down'] = nrm((DEPTH, D_FF, D_MODEL), D_FF ** -0.5)
    d['ffn2_post_g'] = gain(D_MODEL)
    return d


def reference(x_prompt, x_sample, state_conv, cache_k_win, cache_v_win,
              ffn1_pre_g, ffn1_w_up, ffn1_w_down, ffn1_post_g,
              mix_pre_g, w_in, conv_dw_w, conv_dw_b, conv_ln_g, conv_ln_b, w_conv_out,
              attn_sinks, w_attn_out, w_out, mix_post_g,
              ffn2_pre_g, ffn2_w_up, ffn2_w_down, ffn2_post_g):
    weights = dict(ffn1_pre_g=ffn1_pre_g, ffn1_w_up=ffn1_w_up, ffn1_w_down=ffn1_w_down, ffn1_post_g=ffn1_post_g,
                   mix_pre_g=mix_pre_g, w_in=w_in, conv_dw_w=conv_dw_w, conv_dw_b=conv_dw_b,
                   conv_ln_g=conv_ln_g, conv_ln_b=conv_ln_b, w_conv_out=w_conv_out,
                   attn_sinks=attn_sinks, w_attn_out=w_attn_out, w_out=w_out, mix_post_g=mix_post_g,
                   ffn2_pre_g=ffn2_pre_g, ffn2_w_up=ffn2_w_up, ffn2_w_down=ffn2_w_down, ffn2_post_g=ffn2_post_g)
    pos_p = jnp.arange(x_prompt.shape[1])
    pos_s = PAST_LEN + jnp.arange(x_sample.shape[1])
    hp, hs = x_prompt, x_sample
    conv_p, kp, vp, conv_s, ks, vs = [], [], [], [], [], []
    for l in range(DEPTH):
        p = {n: w[l] for n, w in weights.items()}
        prefix_p = jnp.zeros((hp.shape[0], CONV_WIDTH - 1, CONV_CH), hp.dtype)
        hp, c1, k1, v1 = decoder_layer(hp, pos_p, prefix_p, prompt_attend, p)
        kb, vb = cache_k_win[l], cache_v_win[l]
        s_attend = lambda q, k, v, sk, kb=kb, vb=vb: sample_attend(q, k, v, sk, kb, vb)
        hs, c2, k2, v2 = decoder_layer(hs, pos_s, state_conv[l], s_attend, p)
        conv_p.append(c1); kp.append(k1); vp.append(v1)
        conv_s.append(c2); ks.append(k2); vs.append(v2)
    return (hp, hs, jnp.stack(conv_p), jnp.stack(kp), jnp.stack(vp),
            jnp.stack(conv_s), jnp.stack(ks), jnp.stack(vs))
```

```python
import functools
import math

import jax
import jax.numpy as jnp
from jax import lax
from jax.experimental import pallas as pl
from jax.experimental.pallas import tpu as pltpu

D = 1024
N_HEADS = 16
HEAD_DIM = 64
N_KV = 4
GROUP = 4
ROT_DIM = 16
ROPE_THETA = 500000.0
WINDOW = 128
CONV_W = 31
HALO = CONV_W - 1
D_FF = 2816
KV_W = N_KV * HEAD_DIM
EPS = 1e-6
NEG_INF = -1e30
PAST_LEN = 8192
O_GLU_G = D
O_Q = 2 * D
O_K = O_Q + D
O_V = O_K + KV_W
O_GC = O_V + KV_W
O_GA = O_GC + D
IN_COLS = O_GA + D

LANES = 128
SUBLANES = 8
VMEM_LIMIT = 56 * 1024 * 1024

FFN_TM = 512
FFN_CHUNKS = 2
MIX_TQ = 256
CONV_RG = 64
SAMPLE_SB = 16

F32 = jnp.float32
BF16 = jnp.bfloat16


def _dot(a, b):
    return jnp.dot(a, b, preferred_element_type=F32)


def _rmsnorm(x, g):
    r = lax.rsqrt(jnp.mean(x * x, axis=-1, keepdims=True) + EPS)
    return x * r * g


def _resident(shape):
    return pl.BlockSpec(shape, lambda *_: (0,) * len(shape), pipeline_mode=pl.Buffered(1))


def _ffn_kernel(x_ref, pre_ref, wup_ref, wdn_ref, post_ref, o_ref):
    x = x_ref[...]
    h = _rmsnorm(x, pre_ref[...]).astype(BF16)
    ch = D_FF // FFN_CHUNKS
    y = None
    for c in range(FFN_CHUNKS):
        lo = c * ch
        g = _dot(h, wup_ref[:, lo:lo + ch])
        u = _dot(h, wup_ref[:, D_FF + lo:D_FF + lo + ch])
        a = (jax.nn.silu(g) * u).astype(BF16)
        part = _dot(a, wdn_ref[lo:lo + ch, :])
        y = part if y is None else y + part
    o_ref[...] = x + 0.5 * _rmsnorm(y, post_ref[...])


def _ffn(x2d, pre_g, w_up, w_down, post_g):
    n = x2d.shape[0]
    tm = min(FFN_TM, n)
    assert n % tm == 0 and (D_FF // FFN_CHUNKS) % LANES == 0
    return pl.pallas_call(
        _ffn_kernel,
        grid=(n // tm,),
        in_specs=[
            pl.BlockSpec((tm, D), lambda i: (i, 0)),
            _resident((1, D)),
            _resident((D, 2 * D_FF)),
            _resident((D_FF, D)),
            _resident((1, D)),
        ],
        out_specs=pl.BlockSpec((tm, D), lambda i: (i, 0)),
        out_shape=jax.ShapeDtypeStruct((n, D), F32),
        compiler_params=pltpu.CompilerParams(
            dimension_semantics=("arbitrary",), vmem_limit_bytes=VMEM_LIMIT),
        name="ffn",
    )(x2d, pre_g, w_up, w_down, post_g)


def _rope(z, cos, sin_lo, sin_hi):
    half = ROT_DIM // 2
    outs = []
    for g in range(z.shape[-1] // LANES):
        zg = z[:, g * LANES:(g + 1) * LANES]
        outs.append(zg * cos
                    + pltpu.roll(zg, LANES - half, 1) * sin_lo
                    + pltpu.roll(zg, half, 1) * sin_hi)
    return outs


def _rope_tables(pos):
    half = ROT_DIM // 2
    inv = jnp.exp(-math.log(ROPE_THETA) * jnp.arange(0, ROT_DIM, 2, dtype=F32) / ROT_DIM)
    ang = pos.astype(F32)[:, None] * inv[None, :]
    cos, sin = jnp.cos(ang), jnp.sin(ang)
    n = pos.shape[0]
    pad = jnp.zeros((n, HEAD_DIM - ROT_DIM), F32)
    zero = jnp.zeros((n, half), F32)
    cos_h = jnp.concatenate([cos, cos, pad + 1.0], axis=1)
    lo_h = jnp.concatenate([-sin, zero, pad], axis=1)
    hi_h = jnp.concatenate([zero, sin, pad], axis=1)
    rep = LANES // HEAD_DIM
    return tuple(jnp.tile(t, (1, rep)) for t in (cos_h, lo_h, hi_h))


def _conv_phase_taps():
    groups = {}
    for j in range(CONV_W):
        a, ph = divmod(j + 2, SUBLANES)
        groups.setdefault(ph, []).append((a, j))
    return groups


def _ln_silu(c, g, b):
    mu = jnp.mean(c, axis=-1, keepdims=True)
    var = jnp.mean(jnp.square(c - mu), axis=-1, keepdims=True)
    return jax.nn.silu((c - mu) * lax.rsqrt(var + EPS) * g + b)


def _mix_prompt_kernel(x_ref, pre_ref, win_ref, cw_ref, cb_ref, lng_ref, lnb_ref, wco_ref, sinks_ref,
                       wao_ref, wo_ref, post_ref, cos_ref, slo_ref, shi_ref,
                       y_ref, cst_ref, kw_ref, vw_ref,
                       ubuf, cbuf, qbuf, kbuf, vbuf, obuf):
    tq = x_ref.shape[0]
    t = pl.program_id(1)
    last = pl.num_programs(1) - 1

    @pl.when(t == 0)
    def _():
        ubuf[0:32, :] = jnp.zeros((32, D), F32)
        kbuf[0:WINDOW, :] = jnp.zeros((WINDOW, KV_W), BF16)
        vbuf[0:WINDOW, :] = jnp.zeros((WINDOW, KV_W), BF16)

    @pl.when(t > 0)
    def _():
        ubuf[0:32, :] = ubuf[tq:tq + 32, :]
        kbuf[0:WINDOW, :] = kbuf[tq:tq + WINDOW, :]
        vbuf[0:WINDOW, :] = vbuf[tq:tq + WINDOW, :]

    x = x_ref[...]
    h = _rmsnorm(x, pre_ref[...]).astype(BF16)

    u = _dot(h, win_ref[:, 0:D]) * jax.nn.sigmoid(_dot(h, win_ref[:, O_GLU_G:O_GLU_G + D]))
    ubuf[32:32 + tq, :] = u

    @pl.when(t == last)
    def _():
        cst_ref[...] = ubuf[tq + 32 - HALO:tq + 32, :]

    phases = _conv_phase_taps()

    def conv_rows(i, carry):
        base = pl.multiple_of(i * CONV_RG, CONV_RG)
        for lg in range(D // LANES):
            lanes = slice(lg * LANES, (lg + 1) * LANES)
            out = jnp.broadcast_to(cb_ref[:, lanes], (CONV_RG, LANES))
            for ph, taps in phases.items():
                rows = CONV_RG if ph == 0 else CONV_RG + SUBLANES
                p = None
                for a, j in taps:
                    term = (jnp.broadcast_to(cw_ref[j:j + 1, lanes], (rows, LANES))
                            * ubuf[pl.ds(base + SUBLANES * a, rows), lanes])
                    p = term if p is None else p + term
                out = out + p[ph:ph + CONV_RG]
            cbuf[pl.ds(base, CONV_RG), lanes] = out
        return carry

    lax.fori_loop(0, tq // CONV_RG, conv_rows, 0)
    cs = _ln_silu(cbuf[...], lng_ref[...], lnb_ref[...]).astype(BF16)
    conv_out = _dot(cs, wco_ref[...])

    cos, slo, shi = cos_ref[...], slo_ref[...], shi_ref[...]
    zq = _dot(h, win_ref[:, O_Q:O_Q + D])
    for g, rq in enumerate(_rope(zq, cos, slo, shi)):
        qbuf[:, g * LANES:(g + 1) * LANES] = (rq * (HEAD_DIM ** -0.5)).astype(BF16)
    rk = jnp.concatenate(_rope(_dot(h, win_ref[:, O_K:O_K + KV_W]), cos, slo, shi), axis=-1)
    zv = _dot(h, win_ref[:, O_V:O_V + KV_W])
    kbuf[WINDOW:WINDOW + tq, :] = rk.astype(BF16)
    vbuf[WINDOW:WINDOW + tq, :] = zv.astype(BF16)

    @pl.when(t == last)
    def _():
        kw_ref[...] = rk[tq - WINDOW:, :]
        vw_ref[...] = zv[tq - WINDOW:, :]

    def attend_block(i, carry):
        r0 = pl.multiple_of(i * WINDOW, WINDOW)
        row = lax.broadcasted_iota(jnp.int32, (WINDOW, 2 * WINDOW), 0)
        col = lax.broadcasted_iota(jnp.int32, (WINDOW, 2 * WINDOW), 1)
        first_visible = jnp.where(jnp.logical_or(i > 0, t > 0), 0, WINDOW)
        mask = (col > row) & (col <= row + WINDOW) & (col >= first_visible)
        outs = []
        for kv in range(N_KV):
            hl = slice(kv * HEAD_DIM, (kv + 1) * HEAD_DIM)
            k = kbuf[pl.ds(r0, 2 * WINDOW), hl]
            v = vbuf[pl.ds(r0, 2 * WINDOW), hl]
            for g in range(GROUP):
                head = kv * GROUP + g
                q = qbuf[pl.ds(r0, WINDOW), head * HEAD_DIM:(head + 1) * HEAD_DIM]
                s = lax.dot_general(q, k, (((1,), (1,)), ((), ())), preferred_element_type=F32)
                s = jnp.where(mask, s, NEG_INF)
                sink = sinks_ref[head]
                m = jnp.maximum(jnp.max(s, axis=-1, keepdims=True), sink)
                p = jnp.exp(s - m)
                den = jnp.sum(p, axis=-1, keepdims=True) + jnp.exp(sink - m)
                outs.append(_dot(p.astype(BF16), v) * (1.0 / den))
        obuf[pl.ds(r0, WINDOW), :] = jnp.concatenate(outs, axis=-1).astype(BF16)
        return carry

    lax.fori_loop(0, tq // WINDOW, attend_block, 0)
    attn_out = _dot(obuf[...], wao_ref[...])

    g_conv = jax.nn.sigmoid(_dot(h, win_ref[:, O_GC:O_GC + D]))
    g_attn = jax.nn.sigmoid(_dot(h, win_ref[:, O_GA:O_GA + D]))
    y = _dot((g_conv * conv_out + g_attn * attn_out).astype(BF16), wo_ref[...])
    y_ref[...] = x + _rmsnorm(y, post_ref[...])


def _mix_prompt(x, p, tables):
    b, t, _ = x.shape
    tq = MIX_TQ
    assert t % tq == 0 and tq % WINDOW == 0 and tq % CONV_RG == 0
    tile = lambda bi, ti: (bi, ti, 0)
    per_seq = lambda bi, ti: (bi, 0, 0)
    tab = pl.BlockSpec((tq, LANES), lambda bi, ti: (ti, 0))
    return pl.pallas_call(
        _mix_prompt_kernel,
        grid=(b, t // tq),
        in_specs=[
            pl.BlockSpec((None, tq, D), tile),
            _resident((1, D)),
            _resident((D, IN_COLS)),
            _resident((CONV_W, D)),
            _resident((1, D)),
            _resident((1, D)),
            _resident((1, D)),
            _resident((D, D)),
            pl.BlockSpec(memory_space=pltpu.SMEM),
            _resident((D, D)),
            _resident((D, D)),
            _resident((1, D)),
            tab, tab, tab,
        ],
        out_specs=[
            pl.BlockSpec((None, tq, D), tile),
            pl.BlockSpec((None, HALO, D), per_seq),
            pl.BlockSpec((None, WINDOW, KV_W), per_seq),
            pl.BlockSpec((None, WINDOW, KV_W), per_seq),
        ],
        out_shape=[
            jax.ShapeDtypeStruct((b, t, D), F32),
            jax.ShapeDtypeStruct((b, HALO, D), F32),
            jax.ShapeDtypeStruct((b, WINDOW, KV_W), F32),
            jax.ShapeDtypeStruct((b, WINDOW, KV_W), F32),
        ],
        scratch_shapes=[
            pltpu.VMEM((32 + tq, D), F32),
            pltpu.VMEM((tq, D), F32),
            pltpu.VMEM((tq, D), BF16),
            pltpu.VMEM((WINDOW + tq, KV_W), BF16),
            pltpu.VMEM((WINDOW + tq, KV_W), BF16),
            pltpu.VMEM((tq, D), BF16),
        ],
        compiler_params=pltpu.CompilerParams(
            dimension_semantics=("arbitrary", "arbitrary"), vmem_limit_bytes=VMEM_LIMIT),
        name="mix_prompt",
    )(x, p["mix_pre_g"], p["w_in"], p["conv_dw_w"], p["conv_dw_b"], p["conv_ln_g"], p["conv_ln_b"],
      p["w_conv_out"], p["attn_sinks"], p["w_attn_out"], p["w_out"], p["mix_post_g"], *tables)


def _mix_sample_kernel(x_ref, st_ref, kc_ref, vc_ref, pre_ref, win_ref, cw_ref, cb_ref, lng_ref, lnb_ref,
                       wco_ref, sinkcol_ref, wao_ref, wo_ref, post_ref, cos_ref, slo_ref, shi_ref,
                       y_ref, cst_ref, kw_ref, vw_ref,
                       full, qf, knew, vnew, kfull, vfull, of):
    nt = x_ref.shape[0]
    sb = nt // SUBLANES
    s_new = SUBLANES
    l_buf = kc_ref.shape[1]

    x = x_ref[...]
    h = _rmsnorm(x, pre_ref[...]).astype(BF16)

    u = _dot(h, win_ref[:, 0:D]) * jax.nn.sigmoid(_dot(h, win_ref[:, O_GLU_G:O_GLU_G + D]))
    full[:, 0:SUBLANES, :] = jnp.zeros((sb, SUBLANES, D), F32)
    full[:, 2:32, :] = st_ref[...]
    full[:, 32:40, :] = u.reshape(sb, s_new, D)
    cst_ref[...] = full[:, 40 - HALO:40, :]

    phases = _conv_phase_taps()
    cols = []
    for lg in range(D // LANES):
        lanes = slice(lg * LANES, (lg + 1) * LANES)
        out = jnp.broadcast_to(cb_ref[:, lanes].reshape(1, 1, LANES), (sb, s_new, LANES))
        for ph, taps in phases.items():
            rows = s_new if ph == 0 else 2 * s_new
            p = None
            for a, j in taps:
                term = (jnp.broadcast_to(cw_ref[j:j + 1, lanes].reshape(1, 1, LANES), (sb, rows, LANES))
                        * full[:, SUBLANES * a:SUBLANES * a + rows, lanes])
                p = term if p is None else p + term
            out = out + p[:, ph:ph + s_new, :]
        cols.append(out.reshape(nt, LANES))
    c = jnp.concatenate(cols, axis=-1)
    cs = _ln_silu(c, lng_ref[...], lnb_ref[...]).astype(BF16)
    conv_out = _dot(cs, wco_ref[...])

    cos, slo, shi = cos_ref[...], slo_ref[...], shi_ref[...]
    zq = _dot(h, win_ref[:, O_Q:O_Q + D])
    for g, rq in enumerate(_rope(zq, cos, slo, shi)):
        qf[:, g * LANES:(g + 1) * LANES] = rq * (HEAD_DIM ** -0.5)
    rk = jnp.concatenate(_rope(_dot(h, win_ref[:, O_K:O_K + KV_W]), cos, slo, shi), axis=-1)
    zv = _dot(h, win_ref[:, O_V:O_V + KV_W])
    knew[...] = rk
    vnew[...] = zv
    kw_ref[:, 0:l_buf - s_new, :] = kc_ref[:, s_new:l_buf, :]
    vw_ref[:, 0:l_buf - s_new, :] = vc_ref[:, s_new:l_buf, :]
    kw_ref[:, l_buf - s_new:l_buf, :] = rk.reshape(sb, s_new, KV_W)
    vw_ref[:, l_buf - s_new:l_buf, :] = zv.reshape(sb, s_new, KV_W)

    n_rows = N_HEADS * s_new
    n_keys = kfull.shape[0]
    lane = lax.broadcasted_iota(jnp.int32, (s_new, LANES), 1)
    low_half = lane < HEAD_DIM
    row = lax.broadcasted_iota(jnp.int32, (n_rows, n_keys), 0)
    col = lax.broadcasted_iota(jnp.int32, (n_rows, n_keys), 1)
    tok = jnp.bitwise_and(row, s_new - 1)
    mask = (col <= tok + l_buf) & (col > tok + l_buf - WINDOW)
    sink_col = sinkcol_ref[...]
    zeros8 = jnp.zeros((s_new, LANES), F32)
    pad_rows = n_keys - l_buf - s_new

    def attend_seq(b, carry):
        r0 = pl.multiple_of(b * s_new, s_new)
        q8 = qf[pl.ds(r0, s_new), :]
        blocks = []
        for kv in range(N_KV):
            for g in range(GROUP):
                head = kv * GROUP + g
                src = q8[:, (head // 2) * LANES:(head // 2 + 1) * LANES]
                if head % 2 != kv % 2:
                    src = pltpu.roll(src, HEAD_DIM, 1)
                piece = jnp.where(low_half if kv % 2 == 0 else ~low_half, src, 0.0)
                blocks.append(jnp.concatenate(
                    [piece, zeros8] if kv // 2 == 0 else [zeros8, piece], axis=-1))
        qm = jnp.concatenate(blocks, axis=0).astype(BF16)

        kfull[0:l_buf, :] = kc_ref[b].astype(BF16)
        vfull[0:l_buf, :] = vc_ref[b].astype(BF16)
        kfull[l_buf:n_keys, :] = jnp.concatenate(
            [knew[pl.ds(r0, s_new), :], jnp.zeros((pad_rows, KV_W), F32)], axis=0).astype(BF16)
        vfull[l_buf:n_keys, :] = jnp.concatenate(
            [vnew[pl.ds(r0, s_new), :], jnp.zeros((pad_rows, KV_W), F32)], axis=0).astype(BF16)

        s = lax.dot_general(qm, kfull[...], (((1,), (1,)), ((), ())), preferred_element_type=F32)
        s = jnp.where(mask, s, NEG_INF)
        m = jnp.maximum(jnp.max(s, axis=-1, keepdims=True), sink_col)
        p = jnp.exp(s - m)
        den = jnp.sum(p, axis=-1, keepdims=True) + jnp.exp(sink_col - m)
        o = _dot(p.astype(BF16), vfull[...]) * (1.0 / den)

        outs = []
        for pair in range(N_HEADS // 2):
            kv = (2 * pair) // GROUP
            halves = []
            for head in (2 * pair, 2 * pair + 1):
                blk = o[head * s_new:(head + 1) * s_new, (kv // 2) * LANES:(kv // 2 + 1) * LANES]
                if head % 2 != kv % 2:
                    blk = pltpu.roll(blk, HEAD_DIM, 1)
                halves.append(blk)
            outs.append(jnp.where(low_half, halves[0], halves[1]))
        of[pl.ds(r0, s_new), :] = jnp.concatenate(outs, axis=-1)
        return carry

    lax.fori_loop(0, sb, attend_seq, 0)
    attn_out = _dot(of[...].astype(BF16), wao_ref[...])

    g_conv = jax.nn.sigmoid(_dot(h, win_ref[:, O_GC:O_GC + D]))
    g_attn = jax.nn.sigmoid(_dot(h, win_ref[:, O_GA:O_GA + D]))
    y = _dot((g_conv * conv_out + g_attn * attn_out).astype(BF16), wo_ref[...])
    y_ref[...] = x + _rmsnorm(y, post_ref[...])


def _mix_sample(x2d, state, kc, vc, p, tables):
    nb, l_buf, _ = kc.shape
    s_new = x2d.shape[0] // nb
    assert s_new == SUBLANES and l_buf % 16 == 0 and nb % SAMPLE_SB == 0
    sb = SAMPLE_SB
    nt = sb * s_new
    n_keys = l_buf + 16
    sink_col = jnp.repeat(p["attn_sinks"], s_new).reshape(N_HEADS * s_new, 1)
    rows = lambda i: (i, 0)
    seqs = lambda i: (i, 0, 0)
    tab = pl.BlockSpec((nt, LANES), lambda i: (0, 0), pipeline_mode=pl.Buffered(1))
    return pl.pallas_call(
        _mix_sample_kernel,
        grid=(nb // sb,),
        in_specs=[
            pl.BlockSpec((nt, D), rows),
            pl.BlockSpec((sb, HALO, D), seqs),
            pl.BlockSpec((sb, l_buf, KV_W), seqs),
            pl.BlockSpec((sb, l_buf, KV_W), seqs),
            _resident((1, D)),
            _resident((D, IN_COLS)),
            _resident((CONV_W, D)),
            _resident((1, D)),
            _resident((1, D)),
            _resident((1, D)),
            _resident((D, D)),
            _resident((N_HEADS * s_new, 1)),
            _resident((D, D)),
            _resident((D, D)),
            _resident((1, D)),
            tab, tab, tab,
        ],
        out_specs=[
            pl.BlockSpec((nt, D), rows),
            pl.BlockSpec((sb, HALO, D), seqs),
            pl.BlockSpec((sb, l_buf, KV_W), seqs),
            pl.BlockSpec((sb, l_buf, KV_W), seqs),
        ],
        out_shape=[
            jax.ShapeDtypeStruct(x2d.shape, F32),
            jax.ShapeDtypeStruct((nb, HALO, D), F32),
            jax.ShapeDtypeStruct((nb, l_buf, KV_W), F32),
            jax.ShapeDtypeStruct((nb, l_buf, KV_W), F32),
        ],
        scratch_shapes=[
            pltpu.VMEM((sb, 40, D), F32),
            pltpu.VMEM((nt, D), F32),
            pltpu.VMEM((nt, KV_W), F32),
            pltpu.VMEM((nt, KV_W), F32),
            pltpu.VMEM((n_keys, KV_W), BF16),
            pltpu.VMEM((n_keys, KV_W), BF16),
            pltpu.VMEM((nt, D), F32),
        ],
        compiler_params=pltpu.CompilerParams(
            dimension_semantics=("arbitrary",), vmem_limit_bytes=VMEM_LIMIT),
        name="mix_sample",
    )(x2d, state, kc, vc, p["mix_pre_g"], p["w_in"], p["conv_dw_w"], p["conv_dw_b"], p["conv_ln_g"],
      p["conv_ln_b"], p["w_conv_out"], sink_col, p["w_attn_out"], p["w_out"], p["mix_post_g"], *tables)


def kernel(x_prompt, x_sample, state_conv, cache_k_win, cache_v_win, ffn1_pre_g, ffn1_w_up, ffn1_w_down, ffn1_post_g, mix_pre_g, w_in, conv_dw_w, conv_dw_b, conv_ln_g, conv_ln_b, w_conv_out, attn_sinks, w_attn_out, w_out, mix_post_g, ffn2_pre_g, ffn2_w_up, ffn2_w_down, ffn2_post_g):
    depth = w_in.shape[0]
    b, t, _ = x_prompt.shape
    nb, s_new, _ = x_sample.shape
    l_buf = cache_k_win.shape[2]
    matmul_weights = dict(ffn1_w_up=ffn1_w_up, ffn1_w_down=ffn1_w_down, w_in=w_in, w_conv_out=w_conv_out,
                          w_attn_out=w_attn_out, w_out=w_out, ffn2_w_up=ffn2_w_up, ffn2_w_down=ffn2_w_down)
    other = dict(ffn1_pre_g=ffn1_pre_g, ffn1_post_g=ffn1_post_g, mix_pre_g=mix_pre_g, conv_dw_w=conv_dw_w,
                 conv_dw_b=conv_dw_b, conv_ln_g=conv_ln_g, conv_ln_b=conv_ln_b, attn_sinks=attn_sinks,
                 mix_post_g=mix_post_g, ffn2_pre_g=ffn2_pre_g, ffn2_post_g=ffn2_post_g)
    tab_p = _rope_tables(jnp.arange(t))
    tab_s = _rope_tables(jnp.tile(PAST_LEN + jnp.arange(s_new), SAMPLE_SB))

    hp = x_prompt.reshape(b * t, D)
    hs = x_sample.reshape(nb * s_new, D)
    outs = [[] for _ in range(6)]
    for l in range(depth):
        p = {n: w[l].astype(BF16) for n, w in matmul_weights.items()}
        for n, w in other.items():
            p[n] = w[l] if w[l].ndim == 2 or n == "attn_sinks" else w[l].reshape(1, -1)

        hp = _ffn(hp, p["ffn1_pre_g"], p["ffn1_w_up"], p["ffn1_w_down"], p["ffn1_post_g"])
        hp, c1, k1, v1 = _mix_prompt(hp.reshape(b, t, D), p, tab_p)
        hp = _ffn(hp.reshape(b * t, D), p["ffn2_pre_g"], p["ffn2_w_up"], p["ffn2_w_down"], p["ffn2_post_g"])

        hs = _ffn(hs, p["ffn1_pre_g"], p["ffn1_w_up"], p["ffn1_w_down"], p["ffn1_post_g"])
        hs, c2, k2, v2 = _mix_sample(hs, state_conv[l], cache_k_win[l].reshape(nb, l_buf, KV_W),
                                     cache_v_win[l].reshape(nb, l_buf, KV_W), p, tab_s)
        hs = _ffn(hs, p["ffn2_pre_g"], p["ffn2_w_up"], p["ffn2_w_down"], p["ffn2_post_g"])

        kv_shape = lambda n: (n, -1, N_KV, HEAD_DIM)
        for acc, val in zip(outs, (c1, k1.reshape(kv_shape(b)), v1.reshape(kv_shape(b)),
                                   c2, k2.reshape(kv_shape(nb)), v2.reshape(kv_shape(nb)))):
            acc.append(val)
    return (hp.reshape(b, t, D), hs.reshape(nb, s_new, D), *(jnp.stack(o) for o in outs))
```

```python
import math

import jax
import jax.numpy as jnp
from jax import lax
from jax.experimental import pallas as pl
from jax.experimental.pallas import tpu as pltpu

D = 1024
N_HEADS = 16
HEAD_DIM = 64
N_KV = 4
GROUP = 4
ROT_DIM = 16
ROPE_THETA = 500000.0
WINDOW = 128
CONV_W = 31
HALO = CONV_W - 1
D_FF = 2816
KV_W = N_KV * HEAD_DIM
EPS = 1e-6
NEG_INF = -1e30
PAST_LEN = 8192
O_GLU_G = D
O_Q = 2 * D
O_K = O_Q + D
O_V = O_K + KV_W
O_GC = O_V + KV_W
O_GA = O_GC + D
IN_COLS = O_GA + D

LANES = 128
SUBLANES = 8
VMEM_LIMIT = 56 * 1024 * 1024

FFN_TM = 512
FFN_CHUNKS = 2
MIX_TQ = 256
CONV_STRIDE = 4
SAMPLE_SB = 16

F32 = jnp.float32
BF16 = jnp.bfloat16


def _dot(a, b):
    return jnp.dot(a, b, preferred_element_type=F32)


def _rmsnorm(x, g):
    r = lax.rsqrt(jnp.mean(x * x, axis=-1, keepdims=True) + EPS)
    return x * r * g


def _resident(shape):
    return pl.BlockSpec(shape, lambda *_: (0,) * len(shape), pipeline_mode=pl.Buffered(1))


def _ffn_kernel(x_ref, pre_ref, wup_ref, wdn_ref, post_ref, o_ref):
    x = x_ref[...]
    h = _rmsnorm(x, pre_ref[...]).astype(BF16)
    ch = D_FF // FFN_CHUNKS
    y = None
    for c in range(FFN_CHUNKS):
        lo = c * ch
        g = _dot(h, wup_ref[:, lo:lo + ch])
        u = _dot(h, wup_ref[:, D_FF + lo:D_FF + lo + ch])
        a = (jax.nn.silu(g) * u).astype(BF16)
        part = _dot(a, wdn_ref[lo:lo + ch, :])
        y = part if y is None else y + part
    o_ref[...] = x + 0.5 * _rmsnorm(y, post_ref[...])


def _ffn(x2d, pre_g, w_up, w_down, post_g):
    n = x2d.shape[0]
    tm = min(FFN_TM, n)
    assert n % tm == 0 and (D_FF // FFN_CHUNKS) % LANES == 0
    return pl.pallas_call(
        _ffn_kernel,
        grid=(n // tm,),
        in_specs=[
            pl.BlockSpec((tm, D), lambda i: (i, 0)),
            _resident((1, D)),
            _resident((D, 2 * D_FF)),
            _resident((D_FF, D)),
            _resident((1, D)),
        ],
        out_specs=pl.BlockSpec((tm, D), lambda i: (i, 0)),
        out_shape=jax.ShapeDtypeStruct((n, D), F32),
        compiler_params=pltpu.CompilerParams(
            dimension_semantics=("arbitrary",), vmem_limit_bytes=VMEM_LIMIT),
        name="ffn",
    )(x2d, pre_g, w_up, w_down, post_g)


def _rope(z, cos, sin_lo, sin_hi):
    half = ROT_DIM // 2
    outs = []
    for g in range(z.shape[-1] // LANES):
        zg = z[:, g * LANES:(g + 1) * LANES]
        outs.append(zg * cos
                    + pltpu.roll(zg, LANES - half, 1) * sin_lo
                    + pltpu.roll(zg, half, 1) * sin_hi)
    return outs


def _rope_tables(pos):
    half = ROT_DIM // 2
    inv = jnp.exp(-math.log(ROPE_THETA) * jnp.arange(0, ROT_DIM, 2, dtype=F32) / ROT_DIM)
    ang = pos.astype(F32)[:, None] * inv[None, :]
    cos, sin = jnp.cos(ang), jnp.sin(ang)
    n = pos.shape[0]
    pad = jnp.zeros((n, HEAD_DIM - ROT_DIM), F32)
    zero = jnp.zeros((n, half), F32)
    cos_h = jnp.concatenate([cos, cos, pad + 1.0], axis=1)
    lo_h = jnp.concatenate([-sin, zero, pad], axis=1)
    hi_h = jnp.concatenate([zero, sin, pad], axis=1)
    rep = LANES // HEAD_DIM
    return tuple(jnp.tile(t, (1, rep)) for t in (cos_h, lo_h, hi_h))


def _conv_phase_taps():
    groups = {}
    for j in range(CONV_W):
        a, ph = divmod(j + 2, SUBLANES)
        groups.setdefault(ph, []).append((a, j))
    return groups


def _ln_silu(c, g, b):
    mu = jnp.mean(c, axis=-1, keepdims=True)
    var = jnp.mean(jnp.square(c - mu), axis=-1, keepdims=True)
    return jax.nn.silu((c - mu) * lax.rsqrt(var + EPS) * g + b)


def _interleave(primary, secondary):
    done = 0
    for idx, thunk in enumerate(primary):
        thunk()
        want = ((idx + 1) * len(secondary)) // len(primary)
        while done < want:
            secondary[done]()
            done += 1


def _dup_halves(z, kv):
    src = z[:, (kv // 2) * LANES:(kv // 2 + 1) * LANES]
    other = pltpu.roll(src, HEAD_DIM, 1)
    low = lax.broadcasted_iota(jnp.int32, src.shape, 1) < HEAD_DIM
    return jnp.where(low, src, other) if kv % 2 == 0 else jnp.where(low, other, src)


def _mix_prompt_kernel(x_ref, pre_ref, win_ref, cw_ref, cb_ref, lng_ref, lnb_ref, wco_ref, sinks_ref,
                       wao_ref, wo_ref, post_ref, cos_ref, slo_ref, shi_ref,
                       y_ref, cst_ref, kw_ref, vw_ref,
                       ubuf, cbuf, qbuf, kdup, vdup, gbuf, obuf):
    tq = x_ref.shape[0]
    t = pl.program_id(1)
    n_slab = D // LANES

    @pl.when(t == 0)
    def _():
        ubuf[:, 0:32, :] = jnp.zeros((n_slab, 32, LANES), F32)
        kdup[:, 0:WINDOW, :] = jnp.zeros((N_KV, WINDOW, LANES), BF16)
        vdup[:, 0:WINDOW, :] = jnp.zeros((N_KV, WINDOW, LANES), BF16)

    @pl.when(t > 0)
    def _():
        ubuf[:, 0:32, :] = ubuf[:, tq:tq + 32, :]
        kdup[:, 0:WINDOW, :] = kdup[:, tq:tq + WINDOW, :]
        vdup[:, 0:WINDOW, :] = vdup[:, tq:tq + WINDOW, :]

    x = x_ref[...]
    h = _rmsnorm(x, pre_ref[...]).astype(BF16)

    u = _dot(h, win_ref[:, 0:D]) * jax.nn.sigmoid(_dot(h, win_ref[:, O_GLU_G:O_GLU_G + D]))
    for lg in range(n_slab):
        ubuf[lg, 32:32 + tq, :] = u[:, lg * LANES:(lg + 1) * LANES]
    cst_ref[...] = u[tq - HALO:, :]

    rows_per_unit = CONV_STRIDE * SUBLANES

    def conv_unit(lg, base):
        def emit():
            lanes = slice(lg * LANES, (lg + 1) * LANES)
            w = [jnp.broadcast_to(cw_ref[j:j + 1, lanes], (SUBLANES, LANES)) for j in range(CONV_W)]
            acc = [jnp.broadcast_to(cb_ref[:, lanes], (SUBLANES, LANES))] * CONV_STRIDE
            for off in range(CONV_STRIDE - 1 + CONV_W):
                win = ubuf[lg, pl.ds(base + off + 2, SUBLANES, stride=CONV_STRIDE), :]
                for r in range(CONV_STRIDE):
                    if 0 <= off - r < CONV_W:
                        acc[r] = acc[r] + w[off - r] * win
            for r in range(CONV_STRIDE):
                cbuf[lg, pl.ds(base + r, SUBLANES, stride=CONV_STRIDE), :] = acc[r]
        return emit

    cos, slo, shi = cos_ref[...], slo_ref[...], shi_ref[...]
    chunk = 4 * LANES

    def q_chunk(c):
        def emit():
            z = _dot(h, win_ref[:, O_Q + c * chunk:O_Q + (c + 1) * chunk])
            for g, rq in enumerate(_rope(z, cos, slo, shi)):
                col = c * chunk + g * LANES
                qbuf[:, col:col + LANES] = (rq * (HEAD_DIM ** -0.5)).astype(BF16)
        return emit

    def kv_chunk():
        z = _dot(h, win_ref[:, O_K:O_K + 2 * KV_W])
        rk = jnp.concatenate(_rope(z[:, 0:KV_W], cos, slo, shi), axis=-1)
        zv = z[:, KV_W:2 * KV_W]
        for kv in range(N_KV):
            kdup[kv, WINDOW:WINDOW + tq, :] = _dup_halves(rk, kv).astype(BF16)
            vdup[kv, WINDOW:WINDOW + tq, :] = _dup_halves(zv, kv).astype(BF16)
        kw_ref[...] = rk[tq - WINDOW:, :]
        vw_ref[...] = zv[tq - WINDOW:, :]

    def gate_chunk(c):
        def emit():
            z = _dot(h, win_ref[:, O_GC + c * chunk:O_GC + (c + 1) * chunk])
            gbuf[:, c * chunk:(c + 1) * chunk] = jax.nn.sigmoid(z)
        return emit

    projections = [kv_chunk] + [q_chunk(c) for c in range(D // chunk)] + [gate_chunk(c) for c in range(2 * D // chunk)]
    conv_units = [conv_unit(lg, base) for lg in range(n_slab) for base in range(0, tq, rows_per_unit)]
    _interleave(conv_units, projections)

    c = jnp.concatenate([cbuf[lg] for lg in range(n_slab)], axis=-1)
    cs = _ln_silu(c, lng_ref[...], lnb_ref[...]).astype(BF16)
    conv_out = _dot(cs, wco_ref[...])

    row = lax.broadcasted_iota(jnp.int32, (WINDOW, 2 * WINDOW), 0)
    col = lax.broadcasted_iota(jnp.int32, (WINDOW, 2 * WINDOW), 1)
    band = (col > row) & (col <= row + WINDOW)
    low_q = lax.broadcasted_iota(jnp.int32, (WINDOW, LANES), 1) < HEAD_DIM
    low_kv = lax.broadcasted_iota(jnp.int32, (2 * WINDOW, LANES), 1) < HEAD_DIM
    zero_q = jnp.zeros((WINDOW, LANES), BF16)
    zero_kv = jnp.zeros((2 * WINDOW, LANES), BF16)
    for i in range(tq // WINDOW):
        r0 = i * WINDOW
        mask = band if i > 0 else band & (col >= jnp.where(t > 0, 0, WINDOW))
        for kv in range(N_KV):
            kd = kdup[kv, r0:r0 + 2 * WINDOW, :]
            vd = vdup[kv, r0:r0 + 2 * WINDOW, :]
            v_bd = jnp.concatenate([jnp.where(low_kv, vd, zero_kv), jnp.where(low_kv, zero_kv, vd)], axis=0)
            q_rows = []
            for pair in range(GROUP // 2):
                tile_col = (kv * (GROUP // 2) + pair) * LANES
                qp = qbuf[r0:r0 + WINDOW, tile_col:tile_col + LANES]
                q_rows += [jnp.where(low_q, qp, zero_q), jnp.where(low_q, zero_q, qp)]
            s_all = lax.dot_general(jnp.concatenate(q_rows, axis=0), kd, (((1,), (1,)), ((), ())),
                                    preferred_element_type=F32)
            probs, inv_den = [], []
            for g in range(GROUP):
                s = jnp.where(mask, s_all[g * WINDOW:(g + 1) * WINDOW], NEG_INF)
                sink = sinks_ref[kv * GROUP + g]
                m = jnp.maximum(jnp.max(s, axis=-1, keepdims=True), sink)
                p = jnp.exp(s - m)
                inv_den.append(1.0 / (jnp.sum(p, axis=-1, keepdims=True) + jnp.exp(sink - m)))
                probs.append(p.astype(BF16))
            for pair in range(GROUP // 2):
                tile_col = (kv * (GROUP // 2) + pair) * LANES
                o = _dot(jnp.concatenate(probs[2 * pair:2 * pair + 2], axis=-1), v_bd)
                scale = jnp.where(low_q, inv_den[2 * pair], inv_den[2 * pair + 1])
                obuf[r0:r0 + WINDOW, tile_col:tile_col + LANES] = (o * scale).astype(BF16)
    attn_out = _dot(obuf[...], wao_ref[...])

    merged = gbuf[:, 0:D] * conv_out + gbuf[:, D:2 * D] * attn_out
    y = _dot(merged.astype(BF16), wo_ref[...])
    y_ref[...] = x + _rmsnorm(y, post_ref[...])


def _mix_prompt(x, p, tables):
    b, t, _ = x.shape
    tq = MIX_TQ
    assert t % tq == 0 and tq % WINDOW == 0 and tq % (CONV_STRIDE * SUBLANES) == 0
    tile = lambda bi, ti: (bi, ti, 0)
    per_seq = lambda bi, ti: (bi, 0, 0)
    tab = pl.BlockSpec((tq, LANES), lambda bi, ti: (ti, 0))
    return pl.pallas_call(
        _mix_prompt_kernel,
        grid=(b, t // tq),
        in_specs=[
            pl.BlockSpec((None, tq, D), tile),
            _resident((1, D)),
            _resident((D, IN_COLS)),
            _resident((CONV_W, D)),
            _resident((1, D)),
            _resident((1, D)),
            _resident((1, D)),
            _resident((D, D)),
            pl.BlockSpec(memory_space=pltpu.SMEM),
            _resident((D, D)),
            _resident((D, D)),
            _resident((1, D)),
            tab, tab, tab,
        ],
        out_specs=[
            pl.BlockSpec((None, tq, D), tile),
            pl.BlockSpec((None, HALO, D), per_seq),
            pl.BlockSpec((None, WINDOW, KV_W), per_seq),
            pl.BlockSpec((None, WINDOW, KV_W), per_seq),
        ],
        out_shape=[
            jax.ShapeDtypeStruct((b, t, D), F32),
            jax.ShapeDtypeStruct((b, HALO, D), F32),
            jax.ShapeDtypeStruct((b, WINDOW, KV_W), F32),
            jax.ShapeDtypeStruct((b, WINDOW, KV_W), F32),
        ],
        scratch_shapes=[
            pltpu.VMEM((D // LANES, 32 + tq, LANES), F32),
            pltpu.VMEM((D // LANES, tq, LANES), F32),
            pltpu.VMEM((tq, D), BF16),
            pltpu.VMEM((N_KV, WINDOW + tq, LANES), BF16),
            pltpu.VMEM((N_KV, WINDOW + tq, LANES), BF16),
            pltpu.VMEM((tq, 2 * D), F32),
            pltpu.VMEM((tq, D), BF16),
        ],
        compiler_params=pltpu.CompilerParams(
            dimension_semantics=("arbitrary", "arbitrary"), vmem_limit_bytes=VMEM_LIMIT),
        name="mix_prompt",
    )(x, p["mix_pre_g"], p["w_in"], p["conv_dw_w"], p["conv_dw_b"], p["conv_ln_g"], p["conv_ln_b"],
      p["w_conv_out"], p["attn_sinks"], p["w_attn_out"], p["w_out"], p["mix_post_g"], *tables)


def _mix_sample_kernel(x_ref, st_ref, kc_ref, vc_ref, pre_ref, win_ref, cw_ref, cb_ref, lng_ref, lnb_ref,
                       wco_ref, sinkcol_ref, wao_ref, wo_ref, post_ref, cos_ref, slo_ref, shi_ref,
                       y_ref, cst_ref, kw_ref, vw_ref,
                       full, qf, knew, vnew, kfull, vfull, of):
    nt = x_ref.shape[0]
    sb = nt // SUBLANES
    s_new = SUBLANES
    l_buf = kc_ref.shape[1]

    x = x_ref[...]
    h = _rmsnorm(x, pre_ref[...]).astype(BF16)

    u = _dot(h, win_ref[:, 0:D]) * jax.nn.sigmoid(_dot(h, win_ref[:, O_GLU_G:O_GLU_G + D]))
    full[:, 0:SUBLANES, :] = jnp.zeros((sb, SUBLANES, D), F32)
    full[:, 2:32, :] = st_ref[...]
    full[:, 32:40, :] = u.reshape(sb, s_new, D)
    cst_ref[...] = full[:, 40 - HALO:40, :]

    phases = _conv_phase_taps()
    cols = []
    for lg in range(D // LANES):
        lanes = slice(lg * LANES, (lg + 1) * LANES)
        out = jnp.broadcast_to(cb_ref[:, lanes].reshape(1, 1, LANES), (sb, s_new, LANES))
        for ph, taps in phases.items():
            rows = s_new if ph == 0 else 2 * s_new
            p = None
            for a, j in taps:
                term = (jnp.broadcast_to(cw_ref[j:j + 1, lanes].reshape(1, 1, LANES), (sb, rows, LANES))
                        * full[:, SUBLANES * a:SUBLANES * a + rows, lanes])
                p = term if p is None else p + term
            out = out + p[:, ph:ph + s_new, :]
        cols.append(out.reshape(nt, LANES))
    c = jnp.concatenate(cols, axis=-1)
    cs = _ln_silu(c, lng_ref[...], lnb_ref[...]).astype(BF16)
    conv_out = _dot(cs, wco_ref[...])

    cos, slo, shi = cos_ref[...], slo_ref[...], shi_ref[...]
    zq = _dot(h, win_ref[:, O_Q:O_Q + D])
    for g, rq in enumerate(_rope(zq, cos, slo, shi)):
        qf[:, g * LANES:(g + 1) * LANES] = rq * (HEAD_DIM ** -0.5)
    rk = jnp.concatenate(_rope(_dot(h, win_ref[:, O_K:O_K + KV_W]), cos, slo, shi), axis=-1)
    zv = _dot(h, win_ref[:, O_V:O_V + KV_W])
    knew[...] = rk
    vnew[...] = zv
    kw_ref[:, 0:l_buf - s_new, :] = kc_ref[:, s_new:l_buf, :]
    vw_ref[:, 0:l_buf - s_new, :] = vc_ref[:, s_new:l_buf, :]
    kw_ref[:, l_buf - s_new:l_buf, :] = rk.reshape(sb, s_new, KV_W)
    vw_ref[:, l_buf - s_new:l_buf, :] = zv.reshape(sb, s_new, KV_W)

    n_rows = N_HEADS * s_new
    n_keys = kfull.shape[0]
    lane = lax.broadcasted_iota(jnp.int32, (s_new, LANES), 1)
    low_half = lane < HEAD_DIM
    row = lax.broadcasted_iota(jnp.int32, (n_rows, n_keys), 0)
    col = lax.broadcasted_iota(jnp.int32, (n_rows, n_keys), 1)
    tok = jnp.bitwise_and(row, s_new - 1)
    mask = (col <= tok + l_buf) & (col > tok + l_buf - WINDOW)
    sink_col = sinkcol_ref[...]
    zeros8 = jnp.zeros((s_new, LANES), F32)
    pad_rows = n_keys - l_buf - s_new

    def attend_seq(b, carry):
        r0 = pl.multiple_of(b * s_new, s_new)
        q8 = qf[pl.ds(r0, s_new), :]
        blocks = []
        for kv in range(N_KV):
            for g in range(GROUP):
                head = kv * GROUP + g
                src = q8[:, (head // 2) * LANES:(head // 2 + 1) * LANES]
                if head % 2 != kv % 2:
                    src = pltpu.roll(src, HEAD_DIM, 1)
                piece = jnp.where(low_half if kv % 2 == 0 else ~low_half, src, 0.0)
                blocks.append(jnp.concatenate(
                    [piece, zeros8] if kv // 2 == 0 else [zeros8, piece], axis=-1))
        qm = jnp.concatenate(blocks, axis=0).astype(BF16)

        kfull[0:l_buf, :] = kc_ref[b].astype(BF16)
        vfull[0:l_buf, :] = vc_ref[b].astype(BF16)
        kfull[l_buf:n_keys, :] = jnp.concatenate(
            [knew[pl.ds(r0, s_new), :], jnp.zeros((pad_rows, KV_W), F32)], axis=0).astype(BF16)
        vfull[l_buf:n_keys, :] = jnp.concatenate(
            [vnew[pl.ds(r0, s_new), :], jnp.zeros((pad_rows, KV_W), F32)], axis=0).astype(BF16)

        s = lax.dot_general(qm, kfull[...], (((1,), (1,)), ((), ())), preferred_element_type=F32)
        s = jnp.where(mask, s, NEG_INF)
        m = jnp.maximum(jnp.max(s, axis=-1, keepdims=True), sink_col)
        p = jnp.exp(s - m)
        den = jnp.sum(p, axis=-1, keepdims=True) + jnp.exp(sink_col - m)
        o = _dot(p.astype(BF16), vfull[...]) * (1.0 / den)

        outs = []
        for pair in range(N_HEADS // 2):
            kv = (2 * pair) // GROUP
            halves = []
            for head in (2 * pair, 2 * pair + 1):
                blk = o[head * s_new:(head + 1) * s_new, (kv // 2) * LANES:(kv // 2 + 1) * LANES]
                if head % 2 != kv % 2:
                    blk = pltpu.roll(blk, HEAD_DIM, 1)
                halves.append(blk)
            outs.append(jnp.where(low_half, halves[0], halves[1]))
        of[pl.ds(r0, s_new), :] = jnp.concatenate(outs, axis=-1)
        return carry

    lax.fori_loop(0, sb, attend_seq, 0)
    attn_out = _dot(of[...].astype(BF16), wao_ref[...])

    g_conv = jax.nn.sigmoid(_dot(h, win_ref[:, O_GC:O_GC + D]))
    g_attn = jax.nn.sigmoid(_dot(h, win_ref[:, O_GA:O_GA + D]))
    y = _dot((g_conv * conv_out + g_attn * attn_out).astype(BF16), wo_ref[...])
    y_ref[...] = x + _rmsnorm(y, post_ref[...])


def _mix_sample(x2d, state, kc, vc, p, tables):
    nb, l_buf, _ = kc.shape
    s_new = x2d.shape[0] // nb
    assert s_new == SUBLANES and l_buf % 16 == 0 and nb % SAMPLE_SB == 0
    sb = SAMPLE_SB
    nt = sb * s_new
    n_keys = l_buf + 16
    sink_col = jnp.repeat(p["attn_sinks"], s_new).reshape(N_HEADS * s_new, 1)
    rows = lambda i: (i, 0)
    seqs = lambda i: (i, 0, 0)
    tab = pl.BlockSpec((nt, LANES), lambda i: (0, 0), pipeline_mode=pl.Buffered(1))
    return pl.pallas_call(
        _mix_sample_kernel,
        grid=(nb // sb,),
        in_specs=[
            pl.BlockSpec((nt, D), rows),
            pl.BlockSpec((sb, HALO, D), seqs),
            pl.BlockSpec((sb, l_buf, KV_W), seqs),
            pl.BlockSpec((sb, l_buf, KV_W), seqs),
            _resident((1, D)),
            _resident((D, IN_COLS)),
            _resident((CONV_W, D)),
            _resident((1, D)),
            _resident((1, D)),
            _resident((1, D)),
            _resident((D, D)),
            _resident((N_HEADS * s_new, 1)),
            _resident((D, D)),
            _resident((D, D)),
            _resident((1, D)),
            tab, tab, tab,
        ],
        out_specs=[
            pl.BlockSpec((nt, D), rows),
            pl.BlockSpec((sb, HALO, D), seqs),
            pl.BlockSpec((sb, l_buf, KV_W), seqs),
            pl.BlockSpec((sb, l_buf, KV_W), seqs),
        ],
        out_shape=[
            jax.ShapeDtypeStruct(x2d.shape, F32),
            jax.ShapeDtypeStruct((nb, HALO, D), F32),
            jax.ShapeDtypeStruct((nb, l_buf, KV_W), F32),
            jax.ShapeDtypeStruct((nb, l_buf, KV_W), F32),
        ],
        scratch_shapes=[
            pltpu.VMEM((sb, 40, D), F32),
            pltpu.VMEM((nt, D), F32),
            pltpu.VMEM((nt, KV_W), F32),
            pltpu.VMEM((nt, KV_W), F32),
            pltpu.VMEM((n_keys, KV_W), BF16),
            pltpu.VMEM((n_keys, KV_W), BF16),
            pltpu.VMEM((nt, D), F32),
        ],
        compiler_params=pltpu.CompilerParams(
            dimension_semantics=("arbitrary",), vmem_limit_bytes=VMEM_LIMIT),
        name="mix_sample",
    )(x2d, state, kc, vc, p["mix_pre_g"], p["w_in"], p["conv_dw_w"], p["conv_dw_b"], p["conv_ln_g"],
      p["conv_ln_b"], p["w_conv_out"], sink_col, p["w_attn_out"], p["w_out"], p["mix_post_g"], *tables)


def kernel(x_prompt, x_sample, state_conv, cache_k_win, cache_v_win, ffn1_pre_g, ffn1_w_up, ffn1_w_down, ffn1_post_g, mix_pre_g, w_in, conv_dw_w, conv_dw_b, conv_ln_g, conv_ln_b, w_conv_out, attn_sinks, w_attn_out, w_out, mix_post_g, ffn2_pre_g, ffn2_w_up, ffn2_w_down, ffn2_post_g):
    depth = w_in.shape[0]
    b, t, _ = x_prompt.shape
    nb, s_new, _ = x_sample.shape
    l_buf = cache_k_win.shape[2]
    matmul_weights = dict(ffn1_w_up=ffn1_w_up, ffn1_w_down=ffn1_w_down, w_in=w_in, w_conv_out=w_conv_out,
                          w_attn_out=w_attn_out, w_out=w_out, ffn2_w_up=ffn2_w_up, ffn2_w_down=ffn2_w_down)
    other = dict(ffn1_pre_g=ffn1_pre_g, ffn1_post_g=ffn1_post_g, mix_pre_g=mix_pre_g, conv_dw_w=conv_dw_w,
                 conv_dw_b=conv_dw_b, conv_ln_g=conv_ln_g, conv_ln_b=conv_ln_b, attn_sinks=attn_sinks,
                 mix_post_g=mix_post_g, ffn2_pre_g=ffn2_pre_g, ffn2_post_g=ffn2_post_g)
    tab_p = _rope_tables(jnp.arange(t))
    tab_s = _rope_tables(jnp.tile(PAST_LEN + jnp.arange(s_new), SAMPLE_SB))

    hp = x_prompt.reshape(b * t, D)
    hs = x_sample.reshape(nb * s_new, D)
    outs = [[] for _ in range(6)]
    for l in range(depth):
        p = {n: w[l].astype(BF16) for n, w in matmul_weights.items()}
        for n, w in other.items():
            p[n] = w[l] if w[l].ndim == 2 or n == "attn_sinks" else w[l].reshape(1, -1)

        hp = _ffn(hp, p["ffn1_pre_g"], p["ffn1_w_up"], p["ffn1_w_down"], p["ffn1_post_g"])
        hp, c1, k1, v1 = _mix_prompt(hp.reshape(b, t, D), p, tab_p)
        hp = _ffn(hp.reshape(b * t, D), p["ffn2_pre_g"], p["ffn2_w_up"], p["ffn2_w_down"], p["ffn2_post_g"])

        hs = _ffn(hs, p["ffn1_pre_g"], p["ffn1_w_up"], p["ffn1_w_down"], p["ffn1_post_g"])
        hs, c2, k2, v2 = _mix_sample(hs, state_conv[l], cache_k_win[l].reshape(nb, l_buf, KV_W),
                                     cache_v_win[l].reshape(nb, l_buf, KV_W), p, tab_s)
        hs = _ffn(hs, p["ffn2_pre_g"], p["ffn2_w_up"], p["ffn2_w_down"], p["ffn2_post_g"])

        kv_shape = lambda n: (n, -1, N_KV, HEAD_DIM)
        for acc, val in zip(outs, (c1, k1.reshape(kv_shape(b)), v1.reshape(kv_shape(b)),
                                   c2, k2.reshape(kv_shape(nb)), v2.reshape(kv_shape(nb)))):
            acc.append(val)
    return (hp.reshape(b, t, D), hs.reshape(nb, s_new, D), *(jnp.stack(o) for o in outs))
```

```python
import math

import jax
import jax.numpy as jnp
from jax import lax
from jax.experimental import pallas as pl
from jax.experimental.pallas import tpu as pltpu

D = 1024
N_HEADS = 16
HEAD_DIM = 64
N_KV = 4
GROUP = 4
ROT_DIM = 16
ROPE_THETA = 500000.0
WINDOW = 128
CONV_W = 31
HALO = CONV_W - 1
D_FF = 2816
KV_W = N_KV * HEAD_DIM
EPS = 1e-6
NEG_INF = -1e30
PAST_LEN = 8192
O_GLU_G = D
O_Q = 2 * D
O_K = O_Q + D
O_V = O_K + KV_W
O_GC = O_V + KV_W
O_GA = O_GC + D
IN_COLS = O_GA + D

LANES = 128
SUBLANES = 8
VMEM_LIMIT = 56 * 1024 * 1024

FFN_TM = 512
FFN_CHUNKS = 2
MIX_TQ = 256
CONV_STRIDE = 4
SAMPLE_SB = 16

F32 = jnp.float32
BF16 = jnp.bfloat16


def _dot(a, b):
    return jnp.dot(a, b, preferred_element_type=F32)


def _rmsnorm(x, g):
    r = lax.rsqrt(jnp.mean(x * x, axis=-1, keepdims=True) + EPS)
    return x * r * g


def _resident(shape):
    return pl.BlockSpec(shape, lambda *_: (0,) * len(shape), pipeline_mode=pl.Buffered(1))


def _ffn_kernel(x_ref, pre_ref, wup_ref, wdn_ref, post_ref, o_ref):
    x = x_ref[...]
    h = _rmsnorm(x, pre_ref[...]).astype(BF16)
    ch = D_FF // FFN_CHUNKS
    y = None
    for c in range(FFN_CHUNKS):
        lo = c * ch
        g = _dot(h, wup_ref[:, lo:lo + ch])
        u = _dot(h, wup_ref[:, D_FF + lo:D_FF + lo + ch])
        a = (jax.nn.silu(g) * u).astype(BF16)
        part = _dot(a, wdn_ref[lo:lo + ch, :])
        y = part if y is None else y + part
    o_ref[...] = x + 0.5 * _rmsnorm(y, post_ref[...])


def _ffn(x2d, pre_g, w_up, w_down, post_g):
    n = x2d.shape[0]
    tm = min(FFN_TM, n)
    assert n % tm == 0 and (D_FF // FFN_CHUNKS) % LANES == 0
    return pl.pallas_call(
        _ffn_kernel,
        grid=(n // tm,),
        in_specs=[
            pl.BlockSpec((tm, D), lambda i: (i, 0)),
            _resident((1, D)),
            _resident((D, 2 * D_FF)),
            _resident((D_FF, D)),
            _resident((1, D)),
        ],
        out_specs=pl.BlockSpec((tm, D), lambda i: (i, 0)),
        out_shape=jax.ShapeDtypeStruct((n, D), F32),
        compiler_params=pltpu.CompilerParams(
            dimension_semantics=("arbitrary",), vmem_limit_bytes=VMEM_LIMIT),
        name="ffn",
    )(x2d, pre_g, w_up, w_down, post_g)


def _rope(z, cos, sin_lo, sin_hi):
    half = ROT_DIM // 2
    outs = []
    for g in range(z.shape[-1] // LANES):
        zg = z[:, g * LANES:(g + 1) * LANES]
        outs.append(zg * cos
                    + pltpu.roll(zg, LANES - half, 1) * sin_lo
                    + pltpu.roll(zg, half, 1) * sin_hi)
    return outs


def _rope_tables(pos):
    half = ROT_DIM // 2
    inv = jnp.exp(-math.log(ROPE_THETA) * jnp.arange(0, ROT_DIM, 2, dtype=F32) / ROT_DIM)
    ang = pos.astype(F32)[:, None] * inv[None, :]
    cos, sin = jnp.cos(ang), jnp.sin(ang)
    n = pos.shape[0]
    pad = jnp.zeros((n, HEAD_DIM - ROT_DIM), F32)
    zero = jnp.zeros((n, half), F32)
    cos_h = jnp.concatenate([cos, cos, pad + 1.0], axis=1)
    lo_h = jnp.concatenate([-sin, zero, pad], axis=1)
    hi_h = jnp.concatenate([zero, sin, pad], axis=1)
    rep = LANES // HEAD_DIM
    return tuple(jnp.tile(t, (1, rep)) for t in (cos_h, lo_h, hi_h))


def _conv_phase_taps():
    groups = {}
    for j in range(CONV_W):
        a, ph = divmod(j + 2, SUBLANES)
        groups.setdefault(ph, []).append((a, j))
    return groups


def _ln_silu(c, g, b):
    mu = jnp.mean(c, axis=-1, keepdims=True)
    var = jnp.mean(jnp.square(c - mu), axis=-1, keepdims=True)
    return jax.nn.silu((c - mu) * lax.rsqrt(var + EPS) * g + b)


def _interleave(primary, secondary):
    done = 0
    for idx, thunk in enumerate(primary):
        thunk()
        want = ((idx + 1) * len(secondary)) // len(primary)
        while done < want:
            secondary[done]()
            done += 1


def _dup_halves(z, kv):
    src = z[:, (kv // 2) * LANES:(kv // 2 + 1) * LANES]
    other = pltpu.roll(src, HEAD_DIM, 1)
    low = lax.broadcasted_iota(jnp.int32, src.shape, 1) < HEAD_DIM
    return jnp.where(low, src, other) if kv % 2 == 0 else jnp.where(low, other, src)


def _mix_prompt_kernel(x_ref, pre_ref, win_ref, cw_ref, cb_ref, lng_ref, lnb_ref, wco_ref, sinks_ref,
                       wao_ref, wo_ref, post_ref, cos_ref, slo_ref, shi_ref,
                       y_ref, cst_ref, kw_ref, vw_ref,
                       ubuf, cbuf, qbuf, kdup, vdup, gbuf, obuf):
    tq = x_ref.shape[0]
    t = pl.program_id(1)
    n_slab = D // LANES

    @pl.when(t == 0)
    def _():
        ubuf[:, 0:32, :] = jnp.zeros((n_slab, 32, LANES), F32)
        kdup[:, 0:WINDOW, :] = jnp.zeros((N_KV, WINDOW, LANES), BF16)
        vdup[:, 0:WINDOW, :] = jnp.zeros((N_KV, WINDOW, LANES), BF16)

    @pl.when(t > 0)
    def _():
        ubuf[:, 0:32, :] = ubuf[:, tq:tq + 32, :]
        kdup[:, 0:WINDOW, :] = kdup[:, tq:tq + WINDOW, :]
        vdup[:, 0:WINDOW, :] = vdup[:, tq:tq + WINDOW, :]

    x = x_ref[...]
    h = _rmsnorm(x, pre_ref[...]).astype(BF16)

    u = _dot(h, win_ref[:, 0:D]) * jax.nn.sigmoid(_dot(h, win_ref[:, O_GLU_G:O_GLU_G + D]))
    for lg in range(n_slab):
        ubuf[lg, 32:32 + tq, :] = u[:, lg * LANES:(lg + 1) * LANES]
    cst_ref[...] = u[tq - HALO:, :]

    rows_per_unit = CONV_STRIDE * SUBLANES

    def conv_unit(lg, base):
        def emit():
            lanes = slice(lg * LANES, (lg + 1) * LANES)
            w = [jnp.broadcast_to(cw_ref[j:j + 1, lanes], (SUBLANES, LANES)) for j in range(CONV_W)]
            acc = [jnp.broadcast_to(cb_ref[:, lanes], (SUBLANES, LANES))] * CONV_STRIDE
            for off in range(CONV_STRIDE - 1 + CONV_W):
                win = ubuf[lg, pl.ds(base + off + 2, SUBLANES, stride=CONV_STRIDE), :]
                for r in range(CONV_STRIDE):
                    if 0 <= off - r < CONV_W:
                        acc[r] = acc[r] + w[off - r] * win
            for r in range(CONV_STRIDE):
                cbuf[lg, pl.ds(base + r, SUBLANES, stride=CONV_STRIDE), :] = acc[r]
        return emit

    cos, slo, shi = cos_ref[...], slo_ref[...], shi_ref[...]
    chunk = 4 * LANES

    def q_chunk(c):
        def emit():
            z = _dot(h, win_ref[:, O_Q + c * chunk:O_Q + (c + 1) * chunk])
            for g, rq in enumerate(_rope(z, cos, slo, shi)):
                col = c * chunk + g * LANES
                qbuf[:, col:col + LANES] = (rq * (HEAD_DIM ** -0.5)).astype(BF16)
        return emit

    def kv_chunk():
        z = _dot(h, win_ref[:, O_K:O_K + 2 * KV_W])
        rk = jnp.concatenate(_rope(z[:, 0:KV_W], cos, slo, shi), axis=-1)
        zv = z[:, KV_W:2 * KV_W]
        for kv in range(N_KV):
            kdup[kv, WINDOW:WINDOW + tq, :] = _dup_halves(rk, kv).astype(BF16)
            vdup[kv, WINDOW:WINDOW + tq, :] = _dup_halves(zv, kv).astype(BF16)
        kw_ref[...] = rk[tq - WINDOW:, :]
        vw_ref[...] = zv[tq - WINDOW:, :]

    def gate_chunk(c):
        def emit():
            z = _dot(h, win_ref[:, O_GC + c * chunk:O_GC + (c + 1) * chunk])
            gbuf[:, c * chunk:(c + 1) * chunk] = jax.nn.sigmoid(z)
        return emit

    projections = [kv_chunk] + [q_chunk(c) for c in range(D // chunk)] + [gate_chunk(c) for c in range(2 * D // chunk)]
    conv_units = [conv_unit(lg, base) for lg in range(n_slab) for base in range(0, tq, rows_per_unit)]
    _interleave(conv_units, projections)

    c = jnp.concatenate([cbuf[lg] for lg in range(n_slab)], axis=-1)
    cs = _ln_silu(c, lng_ref[...], lnb_ref[...]).astype(BF16)
    conv_out = _dot(cs, wco_ref[...])

    row = lax.broadcasted_iota(jnp.int32, (WINDOW, 2 * WINDOW), 0)
    col = lax.broadcasted_iota(jnp.int32, (WINDOW, 2 * WINDOW), 1)
    band = (col > row) & (col <= row + WINDOW)
    low_q = lax.broadcasted_iota(jnp.int32, (WINDOW, LANES), 1) < HEAD_DIM
    low_kv = lax.broadcasted_iota(jnp.int32, (2 * WINDOW, LANES), 1) < HEAD_DIM
    zero_q = jnp.zeros((WINDOW, LANES), BF16)
    zero_kv = jnp.zeros((2 * WINDOW, LANES), BF16)
    for i in range(tq // WINDOW):
        r0 = i * WINDOW
        mask = band if i > 0 else band & (col >= jnp.where(t > 0, 0, WINDOW))
        for kv in range(N_KV):
            kd = kdup[kv, r0:r0 + 2 * WINDOW, :]
            vd = vdup[kv, r0:r0 + 2 * WINDOW, :]
            v_bd = jnp.concatenate([jnp.where(low_kv, vd, zero_kv), jnp.where(low_kv, zero_kv, vd)], axis=0)
            q_rows = []
            for pair in range(GROUP // 2):
                tile_col = (kv * (GROUP // 2) + pair) * LANES
                qp = qbuf[r0:r0 + WINDOW, tile_col:tile_col + LANES]
                q_rows += [jnp.where(low_q, qp, zero_q), jnp.where(low_q, zero_q, qp)]
            s_all = lax.dot_general(jnp.concatenate(q_rows, axis=0), kd, (((1,), (1,)), ((), ())),
                                    preferred_element_type=F32)
            probs, inv_den = [], []
            for g in range(GROUP):
                s = jnp.where(mask, s_all[g * WINDOW:(g + 1) * WINDOW], NEG_INF)
                sink = sinks_ref[kv * GROUP + g]
                m = jnp.maximum(jnp.max(s, axis=-1, keepdims=True), sink)
                p = jnp.exp(s - m)
                inv_den.append(1.0 / (jnp.sum(p, axis=-1, keepdims=True) + jnp.exp(sink - m)))
                probs.append(p.astype(BF16))
            for pair in range(GROUP // 2):
                tile_col = (kv * (GROUP // 2) + pair) * LANES
                o = _dot(jnp.concatenate(probs[2 * pair:2 * pair + 2], axis=-1), v_bd)
                scale = jnp.where(low_q, inv_den[2 * pair], inv_den[2 * pair + 1])
                obuf[r0:r0 + WINDOW, tile_col:tile_col + LANES] = (o * scale).astype(BF16)
    attn_out = _dot(obuf[...], wao_ref[...])

    merged = gbuf[:, 0:D] * conv_out + gbuf[:, D:2 * D] * attn_out
    y = _dot(merged.astype(BF16), wo_ref[...])
    y_ref[...] = x + _rmsnorm(y, post_ref[...])


def _mix_prompt(x, p, tables):
    b, t, _ = x.shape
    tq = MIX_TQ
    assert t % tq == 0 and tq % WINDOW == 0 and tq % (CONV_STRIDE * SUBLANES) == 0
    tile = lambda bi, ti: (bi, ti, 0)
    per_seq = lambda bi, ti: (bi, 0, 0)
    tab = pl.BlockSpec((tq, LANES), lambda bi, ti: (ti, 0))
    return pl.pallas_call(
        _mix_prompt_kernel,
        grid=(b, t // tq),
        in_specs=[
            pl.BlockSpec((None, tq, D), tile),
            _resident((1, D)),
            _resident((D, IN_COLS)),
            _resident((CONV_W, D)),
            _resident((1, D)),
            _resident((1, D)),
            _resident((1, D)),
            _resident((D, D)),
            pl.BlockSpec(memory_space=pltpu.SMEM),
            _resident((D, D)),
            _resident((D, D)),
            _resident((1, D)),
            tab, tab, tab,
        ],
        out_specs=[
            pl.BlockSpec((None, tq, D), tile),
            pl.BlockSpec((None, HALO, D), per_seq),
            pl.BlockSpec((None, WINDOW, KV_W), per_seq),
            pl.BlockSpec((None, WINDOW, KV_W), per_seq),
        ],
        out_shape=[
            jax.ShapeDtypeStruct((b, t, D), F32),
            jax.ShapeDtypeStruct((b, HALO, D), F32),
            jax.ShapeDtypeStruct((b, WINDOW, KV_W), F32),
            jax.ShapeDtypeStruct((b, WINDOW, KV_W), F32),
        ],
        scratch_shapes=[
            pltpu.VMEM((D // LANES, 32 + tq, LANES), F32),
            pltpu.VMEM((D // LANES, tq, LANES), F32),
            pltpu.VMEM((tq, D), BF16),
            pltpu.VMEM((N_KV, WINDOW + tq, LANES), BF16),
            pltpu.VMEM((N_KV, WINDOW + tq, LANES), BF16),
            pltpu.VMEM((tq, 2 * D), F32),
            pltpu.VMEM((tq, D), BF16),
        ],
        compiler_params=pltpu.CompilerParams(
            dimension_semantics=("arbitrary", "arbitrary"), vmem_limit_bytes=VMEM_LIMIT),
        name="mix_prompt",
    )(x, p["mix_pre_g"], p["w_in"], p["conv_dw_w"], p["conv_dw_b"], p["conv_ln_g"], p["conv_ln_b"],
      p["w_conv_out"], p["attn_sinks"], p["w_attn_out"], p["w_out"], p["mix_post_g"], *tables)


def _mix_sample_kernel(x_ref, st_ref, kt_ref, vt_ref, pre_ref, win_ref, cw_ref, cb_ref, lng_ref, lnb_ref,
                       wco_ref, sinkcol_ref, wao_ref, wo_ref, post_ref, cos_ref, slo_ref, shi_ref,
                       y_ref, cst_ref, kto_ref, vto_ref,
                       ubuf, cbuf, qf, qm, sbuf, pbuf, obuf, of):
    nt = x_ref.shape[0]
    s_new = SUBLANES
    sb = nt // s_new
    l_buf = kt_ref.shape[2]
    n_slab = D // LANES

    x = x_ref[...]
    h = _rmsnorm(x, pre_ref[...]).astype(BF16)

    u = _dot(h, win_ref[:, 0:D]) * jax.nn.sigmoid(_dot(h, win_ref[:, O_GLU_G:O_GLU_G + D]))
    for lg in range(n_slab):
        ubuf[lg] = u[:, lg * LANES:(lg + 1) * LANES]

    def conv_row(i, lg):
        if i < HALO:
            return st_ref[i, :, lg * LANES:(lg + 1) * LANES]
        return ubuf[lg, pl.ds(i - HALO, sb, stride=s_new), :]

    for lg in range(n_slab):
        lanes = slice(lg * LANES, (lg + 1) * LANES)
        w = [jnp.broadcast_to(cw_ref[j:j + 1, lanes], (sb, LANES)) for j in range(CONV_W)]
        acc = [jnp.broadcast_to(cb_ref[:, lanes], (sb, LANES))] * s_new
        for i in range(HALO + s_new):
            row_i = conv_row(i, lg)
            for tok in range(s_new):
                if 0 <= i - tok < CONV_W:
                    acc[tok] = acc[tok] + w[i - tok] * row_i
            if i >= s_new:
                cst_ref[i - s_new, :, lanes] = row_i
        for tok in range(s_new):
            cbuf[lg, pl.ds(tok, sb, stride=s_new), :] = acc[tok]
    c = jnp.concatenate([cbuf[lg] for lg in range(n_slab)], axis=-1)
    cs = _ln_silu(c, lng_ref[...], lnb_ref[...]).astype(BF16)
    conv_out = _dot(cs, wco_ref[...])

    cos, slo, shi = cos_ref[...], slo_ref[...], shi_ref[...]
    zq = _dot(h, win_ref[:, O_Q:O_Q + D])
    for g, rq in enumerate(_rope(zq, cos, slo, shi)):
        qf[:, g * LANES:(g + 1) * LANES] = rq * (HEAD_DIM ** -0.5)
    rk = jnp.concatenate(_rope(_dot(h, win_ref[:, O_K:O_K + KV_W]), cos, slo, shi), axis=-1)
    zv = _dot(h, win_ref[:, O_V:O_V + KV_W])
    k_new, v_new = rk.astype(BF16), zv.astype(BF16)
    k_new_t, v_new_t = rk.T, zv.T

    n_rows = N_HEADS * s_new
    nt_dims = (((1,), (1,)), ((), ()))
    low_half = lax.broadcasted_iota(jnp.int32, (nt, LANES), 1) < HEAD_DIM
    zeros = jnp.zeros((sb, s_new, LANES), F32)

    for head in range(N_HEADS):
        kv = head // GROUP
        src = qf[:, (head // 2) * LANES:(head // 2 + 1) * LANES]
        if head % 2 != kv % 2:
            src = pltpu.roll(src, HEAD_DIM, 1)
        piece = jnp.where(low_half if kv % 2 == 0 else ~low_half, src, 0.0).reshape(sb, s_new, LANES)
        qm[:, head * s_new:(head + 1) * s_new, :] = jnp.concatenate(
            [piece, zeros] if kv // 2 == 0 else [zeros, piece], axis=-1)

    keep_old = lax.broadcasted_iota(jnp.int32, (KV_W, LANES), 1) < l_buf - s_new
    shift = l_buf - s_new

    def new_columns(z_t, b):
        amount = (shift - b * s_new) % LANES
        return pltpu.roll(z_t, amount, 1) if amount else z_t

    for b in range(sb):
        qb = qm[b].astype(BF16)
        kt = kt_ref[b]
        sbuf[b] = jnp.concatenate([_dot(qb, kt.astype(BF16)),
                                   lax.dot_general(qb, k_new, nt_dims, preferred_element_type=F32)], axis=-1)
        kto_ref[b] = jnp.where(keep_old, pltpu.roll(kt, shift, 1), new_columns(k_new_t, b))

    shape3 = (sb, n_rows, l_buf + nt)
    seq = lax.broadcasted_iota(jnp.int32, shape3, 0)
    tok = jnp.bitwise_and(lax.broadcasted_iota(jnp.int32, shape3, 1), s_new - 1)
    col = lax.broadcasted_iota(jnp.int32, shape3, 2)
    first_new = l_buf + seq * s_new
    mask = ((col < l_buf) & (col > tok + l_buf - WINDOW)) | ((col >= first_new) & (col <= first_new + tok))
    s = jnp.where(mask, sbuf[...], NEG_INF)
    sink = sinkcol_ref[...].reshape(1, n_rows, 1)
    m = jnp.maximum(jnp.max(s, axis=-1, keepdims=True), sink)
    p = jnp.exp(s - m)
    den = jnp.sum(p, axis=-1, keepdims=True) + jnp.exp(sink - m)
    pbuf[...] = (p * (1.0 / den)).astype(BF16)

    for b in range(sb):
        vt = vt_ref[b]
        pb = pbuf[b]
        obuf[b] = (lax.dot_general(pb[:, 0:l_buf], vt.astype(BF16), nt_dims, preferred_element_type=F32)
                   + _dot(pb[:, l_buf:], v_new))
        vto_ref[b] = jnp.where(keep_old, pltpu.roll(vt, shift, 1), new_columns(v_new_t, b))

    for pair in range(N_HEADS // 2):
        kv = (2 * pair) // GROUP
        halves = []
        for head in (2 * pair, 2 * pair + 1):
            blk = obuf[:, head * s_new:(head + 1) * s_new, (kv // 2) * LANES:(kv // 2 + 1) * LANES]
            blk = blk.reshape(nt, LANES)
            halves.append(pltpu.roll(blk, HEAD_DIM, 1) if head % 2 != kv % 2 else blk)
        of[:, pair * LANES:(pair + 1) * LANES] = jnp.where(low_half, halves[0], halves[1])
    attn_out = _dot(of[...].astype(BF16), wao_ref[...])

    g_conv = jax.nn.sigmoid(_dot(h, win_ref[:, O_GC:O_GC + D]))
    g_attn = jax.nn.sigmoid(_dot(h, win_ref[:, O_GA:O_GA + D]))
    y = _dot((g_conv * conv_out + g_attn * attn_out).astype(BF16), wo_ref[...])
    y_ref[...] = x + _rmsnorm(y, post_ref[...])


def _mix_sample(x2d, state_t, kt, vt, p, tables):
    nb, _, l_buf = kt.shape
    s_new = x2d.shape[0] // nb
    sb = SAMPLE_SB
    nt = sb * s_new
    assert s_new == SUBLANES and l_buf == LANES and nt == LANES and nb % sb == 0
    sink_col = jnp.repeat(p["attn_sinks"], s_new).reshape(N_HEADS * s_new, 1)
    rows = lambda i: (i, 0)
    seqs = lambda i: (i, 0, 0)
    taps = lambda i: (0, i, 0)
    tab = pl.BlockSpec((nt, LANES), lambda i: (0, 0), pipeline_mode=pl.Buffered(1))
    return pl.pallas_call(
        _mix_sample_kernel,
        grid=(nb // sb,),
        in_specs=[
            pl.BlockSpec((nt, D), rows),
            pl.BlockSpec((HALO, sb, D), taps),
            pl.BlockSpec((sb, KV_W, l_buf), seqs),
            pl.BlockSpec((sb, KV_W, l_buf), seqs),
            _resident((1, D)),
            _resident((D, IN_COLS)),
            _resident((CONV_W, D)),
            _resident((1, D)),
            _resident((1, D)),
            _resident((1, D)),
            _resident((D, D)),
            _resident((N_HEADS * s_new, 1)),
            _resident((D, D)),
            _resident((D, D)),
            _resident((1, D)),
            tab, tab, tab,
        ],
        out_specs=[
            pl.BlockSpec((nt, D), rows),
            pl.BlockSpec((HALO, sb, D), taps),
            pl.BlockSpec((sb, KV_W, l_buf), seqs),
            pl.BlockSpec((sb, KV_W, l_buf), seqs),
        ],
        out_shape=[
            jax.ShapeDtypeStruct(x2d.shape, F32),
            jax.ShapeDtypeStruct((HALO, nb, D), F32),
            jax.ShapeDtypeStruct((nb, KV_W, l_buf), F32),
            jax.ShapeDtypeStruct((nb, KV_W, l_buf), F32),
        ],
        scratch_shapes=[
            pltpu.VMEM((D // LANES, nt, LANES), F32),
            pltpu.VMEM((D // LANES, nt, LANES), F32),
            pltpu.VMEM((nt, D), F32),
            pltpu.VMEM((sb, N_HEADS * s_new, KV_W), F32),
            pltpu.VMEM((sb, N_HEADS * s_new, l_buf + nt), F32),
            pltpu.VMEM((sb, N_HEADS * s_new, l_buf + nt), BF16),
            pltpu.VMEM((sb, N_HEADS * s_new, KV_W), F32),
            pltpu.VMEM((nt, D), F32),
        ],
        compiler_params=pltpu.CompilerParams(
            dimension_semantics=("arbitrary",), vmem_limit_bytes=VMEM_LIMIT),
        name="mix_sample",
    )(x2d, state_t, kt, vt, p["mix_pre_g"], p["w_in"], p["conv_dw_w"], p["conv_dw_b"], p["conv_ln_g"],
      p["conv_ln_b"], p["w_conv_out"], sink_col, p["w_attn_out"], p["w_out"], p["mix_post_g"], *tables)


def kernel(x_prompt, x_sample, state_conv, cache_k_win, cache_v_win, ffn1_pre_g, ffn1_w_up, ffn1_w_down, ffn1_post_g, mix_pre_g, w_in, conv_dw_w, conv_dw_b, conv_ln_g, conv_ln_b, w_conv_out, attn_sinks, w_attn_out, w_out, mix_post_g, ffn2_pre_g, ffn2_w_up, ffn2_w_down, ffn2_post_g):
    depth = w_in.shape[0]
    b, t, _ = x_prompt.shape
    nb, s_new, _ = x_sample.shape
    l_buf = cache_k_win.shape[2]
    matmul_weights = dict(ffn1_w_up=ffn1_w_up, ffn1_w_down=ffn1_w_down, w_in=w_in, w_conv_out=w_conv_out,
                          w_attn_out=w_attn_out, w_out=w_out, ffn2_w_up=ffn2_w_up, ffn2_w_down=ffn2_w_down)
    other = dict(ffn1_pre_g=ffn1_pre_g, ffn1_post_g=ffn1_post_g, mix_pre_g=mix_pre_g, conv_dw_w=conv_dw_w,
                 conv_dw_b=conv_dw_b, conv_ln_g=conv_ln_g, conv_ln_b=conv_ln_b, attn_sinks=attn_sinks,
                 mix_post_g=mix_post_g, ffn2_pre_g=ffn2_pre_g, ffn2_post_g=ffn2_post_g)
    tab_p = _rope_tables(jnp.arange(t))
    tab_s = _rope_tables(jnp.tile(PAST_LEN + jnp.arange(s_new), SAMPLE_SB))

    hp = x_prompt.reshape(b * t, D)
    hs = x_sample.reshape(nb * s_new, D)
    outs = [[] for _ in range(6)]
    for l in range(depth):
        p = {n: w[l].astype(BF16) for n, w in matmul_weights.items()}
        for n, w in other.items():
            p[n] = w[l] if w[l].ndim == 2 or n == "attn_sinks" else w[l].reshape(1, -1)

        hp = _ffn(hp, p["ffn1_pre_g"], p["ffn1_w_up"], p["ffn1_w_down"], p["ffn1_post_g"])
        hp, c1, k1, v1 = _mix_prompt(hp.reshape(b, t, D), p, tab_p)
        hp = _ffn(hp.reshape(b * t, D), p["ffn2_pre_g"], p["ffn2_w_up"], p["ffn2_w_down"], p["ffn2_post_g"])

        rows_last = lambda w: jnp.transpose(w, (0, 2, 3, 1)).reshape(nb, KV_W, l_buf)
        rows_back = lambda w: jnp.transpose(w.reshape(nb, N_KV, HEAD_DIM, l_buf), (0, 3, 1, 2))
        hs = _ffn(hs, p["ffn1_pre_g"], p["ffn1_w_up"], p["ffn1_w_down"], p["ffn1_post_g"])
        hs, c2, k2, v2 = _mix_sample(hs, jnp.transpose(state_conv[l], (1, 0, 2)),
                                     rows_last(cache_k_win[l]), rows_last(cache_v_win[l]), p, tab_s)
        hs = _ffn(hs, p["ffn2_pre_g"], p["ffn2_w_up"], p["ffn2_w_down"], p["ffn2_post_g"])

        kv_shape = lambda n: (n, -1, N_KV, HEAD_DIM)
        for acc, val in zip(outs, (c1, k1.reshape(kv_shape(b)), v1.reshape(kv_shape(b)),
                                   jnp.transpose(c2, (1, 0, 2)), rows_back(k2), rows_back(v2))):
            acc.append(val)
    return (hp.reshape(b, t, D), hs.reshape(nb, s_new, D), *(jnp.stack(o) for o in outs))
```

```python
import math

import jax
import jax.numpy as jnp
from jax import lax
from jax.experimental import pallas as pl
from jax.experimental.pallas import tpu as pltpu

D = 1024
N_HEADS = 16
HEAD_DIM = 64
N_KV = 4
GROUP = 4
ROT_DIM = 16
ROPE_THETA = 500000.0
WINDOW = 128
CONV_W = 31
HALO = CONV_W - 1
D_FF = 2816
KV_W = N_KV * HEAD_DIM
EPS = 1e-6
NEG_INF = -1e30
PAST_LEN = 8192
O_GLU_G = D
O_Q = 2 * D
O_K = O_Q + D
O_V = O_K + KV_W
O_GC = O_V + KV_W
O_GA = O_GC + D
IN_COLS = O_GA + D

LANES = 128
SUBLANES = 8
VMEM_LIMIT = 56 * 1024 * 1024

FFN_TM = 512
FFN_CHUNKS = 1
MIX_TQ = 256
CONV_STRIDE = 4
SAMPLE_SB = 16

F32 = jnp.float32
BF16 = jnp.bfloat16


def _dot(a, b):
    return jnp.dot(a, b, preferred_element_type=F32)


def _rmsnorm(x, g):
    r = lax.rsqrt(jnp.mean(x * x, axis=-1, keepdims=True) + EPS)
    return x * r * g


def _resident(shape):
    return pl.BlockSpec(shape, lambda *_: (0,) * len(shape), pipeline_mode=pl.Buffered(1))


def _ffn_kernel(x_ref, pre_ref, wup_ref, wdn_ref, post_ref, o_ref):
    x = x_ref[...]
    h = _rmsnorm(x, pre_ref[...]).astype(BF16)
    ch = D_FF // FFN_CHUNKS
    y = None
    for c in range(FFN_CHUNKS):
        lo = c * ch
        g = _dot(h, wup_ref[:, lo:lo + ch])
        u = _dot(h, wup_ref[:, D_FF + lo:D_FF + lo + ch])
        a = (jax.nn.silu(g) * u).astype(BF16)
        part = _dot(a, wdn_ref[lo:lo + ch, :])
        y = part if y is None else y + part
    o_ref[...] = x + 0.5 * _rmsnorm(y, post_ref[...])


def _ffn(x2d, pre_g, w_up, w_down, post_g):
    n = x2d.shape[0]
    tm = min(FFN_TM, n)
    assert n % tm == 0 and (D_FF // FFN_CHUNKS) % LANES == 0
    return pl.pallas_call(
        _ffn_kernel,
        grid=(n // tm,),
        in_specs=[
            pl.BlockSpec((tm, D), lambda i: (i, 0)),
            _resident((1, D)),
            _resident((D, 2 * D_FF)),
            _resident((D_FF, D)),
            _resident((1, D)),
        ],
        out_specs=pl.BlockSpec((tm, D), lambda i: (i, 0)),
        out_shape=jax.ShapeDtypeStruct((n, D), F32),
        compiler_params=pltpu.CompilerParams(
            dimension_semantics=("arbitrary",), vmem_limit_bytes=VMEM_LIMIT),
        name="ffn",
    )(x2d, pre_g, w_up, w_down, post_g)


def _rope(z, cos, sin_lo, sin_hi):
    half = ROT_DIM // 2
    outs = []
    for g in range(z.shape[-1] // LANES):
        zg = z[:, g * LANES:(g + 1) * LANES]
        outs.append(zg * cos
                    + pltpu.roll(zg, LANES - half, 1) * sin_lo
                    + pltpu.roll(zg, half, 1) * sin_hi)
    return outs


def _rope_tables(pos):
    half = ROT_DIM // 2
    inv = jnp.exp(-math.log(ROPE_THETA) * jnp.arange(0, ROT_DIM, 2, dtype=F32) / ROT_DIM)
    ang = pos.astype(F32)[:, None] * inv[None, :]
    cos, sin = jnp.cos(ang), jnp.sin(ang)
    n = pos.shape[0]
    pad = jnp.zeros((n, HEAD_DIM - ROT_DIM), F32)
    zero = jnp.zeros((n, half), F32)
    cos_h = jnp.concatenate([cos, cos, pad + 1.0], axis=1)
    lo_h = jnp.concatenate([-sin, zero, pad], axis=1)
    hi_h = jnp.concatenate([zero, sin, pad], axis=1)
    rep = LANES // HEAD_DIM
    return tuple(jnp.tile(t, (1, rep)) for t in (cos_h, lo_h, hi_h))


def _conv_phase_taps():
    groups = {}
    for j in range(CONV_W):
        a, ph = divmod(j + 2, SUBLANES)
        groups.setdefault(ph, []).append((a, j))
    return groups


def _ln_silu(c, g, b):
    mu = jnp.mean(c, axis=-1, keepdims=True)
    var = jnp.mean(jnp.square(c - mu), axis=-1, keepdims=True)
    return jax.nn.silu((c - mu) * lax.rsqrt(var + EPS) * g + b)


def _interleave(primary, secondary):
    done = 0
    for idx, thunk in enumerate(primary):
        thunk()
        want = ((idx + 1) * len(secondary)) // len(primary)
        while done < want:
            secondary[done]()
            done += 1


def _dup_halves(z, kv):
    src = z[:, (kv // 2) * LANES:(kv // 2 + 1) * LANES]
    other = pltpu.roll(src, HEAD_DIM, 1)
    low = lax.broadcasted_iota(jnp.int32, src.shape, 1) < HEAD_DIM
    return jnp.where(low, src, other) if kv % 2 == 0 else jnp.where(low, other, src)


def _mix_prompt_kernel(x_ref, pre_ref, win_ref, cw_ref, cb_ref, lng_ref, lnb_ref, wco_ref, sinks_ref,
                       wao_ref, wo_ref, post_ref, cos_ref, slo_ref, shi_ref,
                       y_ref, cst_ref, kw_ref, vw_ref,
                       ubuf, cbuf, qbuf, kdup, vdup, gbuf, obuf):
    tq = x_ref.shape[0]
    t = pl.program_id(1)
    n_slab = D // LANES

    @pl.when(t == 0)
    def _():
        ubuf[:, 0:32, :] = jnp.zeros((n_slab, 32, LANES), F32)
        kdup[:, 0:WINDOW, :] = jnp.zeros((N_KV, WINDOW, LANES), BF16)
        vdup[:, 0:WINDOW, :] = jnp.zeros((N_KV, WINDOW, LANES), BF16)

    @pl.when(t > 0)
    def _():
        ubuf[:, 0:32, :] = ubuf[:, tq:tq + 32, :]
        kdup[:, 0:WINDOW, :] = kdup[:, tq:tq + WINDOW, :]
        vdup[:, 0:WINDOW, :] = vdup[:, tq:tq + WINDOW, :]

    x = x_ref[...]
    h = _rmsnorm(x, pre_ref[...]).astype(BF16)

    u = _dot(h, win_ref[:, 0:D]) * jax.nn.sigmoid(_dot(h, win_ref[:, O_GLU_G:O_GLU_G + D]))
    for lg in range(n_slab):
        ubuf[lg, 32:32 + tq, :] = u[:, lg * LANES:(lg + 1) * LANES]
    cst_ref[...] = u[tq - HALO:, :]

    rows_per_unit = CONV_STRIDE * SUBLANES

    def conv_unit(lg, base):
        def emit():
            lanes = slice(lg * LANES, (lg + 1) * LANES)
            w = {}
            acc = [jnp.broadcast_to(cb_ref[:, lanes], (SUBLANES, LANES))] * CONV_STRIDE
            for off in range(CONV_STRIDE - 1 + CONV_W):
                win = ubuf[lg, pl.ds(base + off + 2, SUBLANES, stride=CONV_STRIDE), :]
                if off < CONV_W:
                    w[off] = jnp.broadcast_to(cw_ref[off:off + 1, lanes], (SUBLANES, LANES))
                for r in range(CONV_STRIDE):
                    if 0 <= off - r < CONV_W:
                        acc[r] = acc[r] + w[off - r] * win
            for r in range(CONV_STRIDE):
                cbuf[lg, pl.ds(base + r, SUBLANES, stride=CONV_STRIDE), :] = acc[r]
        return emit

    cos, slo, shi = cos_ref[...], slo_ref[...], shi_ref[...]
    chunk = 4 * LANES

    def q_chunk(c):
        def emit():
            z = _dot(h, win_ref[:, O_Q + c * chunk:O_Q + (c + 1) * chunk])
            for g, rq in enumerate(_rope(z, cos, slo, shi)):
                col = c * chunk + g * LANES
                qbuf[:, col:col + LANES] = (rq * (HEAD_DIM ** -0.5)).astype(BF16)
        return emit

    def kv_chunk():
        z = _dot(h, win_ref[:, O_K:O_K + 2 * KV_W])
        rk = jnp.concatenate(_rope(z[:, 0:KV_W], cos, slo, shi), axis=-1)
        zv = z[:, KV_W:2 * KV_W]
        for kv in range(N_KV):
            kdup[kv, WINDOW:WINDOW + tq, :] = _dup_halves(rk, kv).astype(BF16)
            vdup[kv, WINDOW:WINDOW + tq, :] = _dup_halves(zv, kv).astype(BF16)
        kw_ref[...] = rk[tq - WINDOW:, :]
        vw_ref[...] = zv[tq - WINDOW:, :]

    def gate_chunk(c):
        def emit():
            z = _dot(h, win_ref[:, O_GC + c * chunk:O_GC + (c + 1) * chunk])
            gbuf[:, c * chunk:(c + 1) * chunk] = jax.nn.sigmoid(z)
        return emit

    projections = [kv_chunk] + [q_chunk(c) for c in range(D // chunk)] + [gate_chunk(c) for c in range(2 * D // chunk)]
    conv_units = [conv_unit(lg, base) for lg in range(n_slab) for base in range(0, tq, rows_per_unit)]
    _interleave(conv_units, projections)

    c = jnp.concatenate([cbuf[lg] for lg in range(n_slab)], axis=-1)
    cs = _ln_silu(c, lng_ref[...], lnb_ref[...]).astype(BF16)
    conv_out = _dot(cs, wco_ref[...])

    row = lax.broadcasted_iota(jnp.int32, (WINDOW, 2 * WINDOW), 0)
    col = lax.broadcasted_iota(jnp.int32, (WINDOW, 2 * WINDOW), 1)
    band = (col > row) & (col <= row + WINDOW)
    low_q = lax.broadcasted_iota(jnp.int32, (WINDOW, LANES), 1) < HEAD_DIM
    low_kv = lax.broadcasted_iota(jnp.int32, (2 * WINDOW, LANES), 1) < HEAD_DIM
    zero_q = jnp.zeros((WINDOW, LANES), BF16)
    zero_kv = jnp.zeros((2 * WINDOW, LANES), BF16)
    sink_slot = lax.broadcasted_iota(jnp.int32, (WINDOW, LANES), 1) == 0
    not_key0 = lax.broadcasted_iota(jnp.int32, (2 * WINDOW, LANES), 0) > 0
    no_key = jnp.full((WINDOW, LANES), NEG_INF, F32)
    ones_bd = jnp.concatenate([low_kv, ~low_kv], axis=0).astype(F32).astype(BF16)
    for i in range(tq // WINDOW):
        r0 = i * WINDOW
        mask = band if i > 0 else band & (col >= jnp.where(t > 0, 0, WINDOW))
        for kv in range(N_KV):
            kd = kdup[kv, r0:r0 + 2 * WINDOW, :]
            vd = vdup[kv, r0:r0 + 2 * WINDOW, :]
            v_bd = jnp.concatenate([jnp.where(low_kv & not_key0, vd, zero_kv),
                                    jnp.where(~low_kv & not_key0, vd, zero_kv)], axis=0)
            v_aug = jnp.concatenate([v_bd, ones_bd], axis=-1)
            q_rows = []
            for pair in range(GROUP // 2):
                tile_col = (kv * (GROUP // 2) + pair) * LANES
                qp = qbuf[r0:r0 + WINDOW, tile_col:tile_col + LANES]
                q_rows += [jnp.where(low_q, qp, zero_q), jnp.where(low_q, zero_q, qp)]
            s_all = lax.dot_general(jnp.concatenate(q_rows, axis=0), kd, (((1,), (1,)), ((), ())),
                                    preferred_element_type=F32)
            probs = []
            for g in range(GROUP):
                fill = jnp.concatenate([jnp.where(sink_slot, sinks_ref[kv * GROUP + g], no_key), no_key], axis=-1)
                s = jnp.where(mask, s_all[g * WINDOW:(g + 1) * WINDOW], fill)
                probs.append(jnp.exp(s - jnp.max(s, axis=-1, keepdims=True)).astype(BF16))
            for pair in range(GROUP // 2):
                tile_col = (kv * (GROUP // 2) + pair) * LANES
                oa = _dot(jnp.concatenate(probs[2 * pair:2 * pair + 2], axis=-1), v_aug)
                obuf[r0:r0 + WINDOW, tile_col:tile_col + LANES] = (oa[:, 0:LANES] / oa[:, LANES:]).astype(BF16)
    attn_out = _dot(obuf[...], wao_ref[...])

    merged = gbuf[:, 0:D] * conv_out + gbuf[:, D:2 * D] * attn_out
    y = _dot(merged.astype(BF16), wo_ref[...])
    y_ref[...] = x + _rmsnorm(y, post_ref[...])


def _mix_prompt(x, p, tables):
    b, t, _ = x.shape
    tq = MIX_TQ
    assert t % tq == 0 and tq % WINDOW == 0 and tq % (CONV_STRIDE * SUBLANES) == 0
    tile = lambda bi, ti: (bi, ti, 0)
    per_seq = lambda bi, ti: (bi, 0, 0)
    tab = pl.BlockSpec((tq, LANES), lambda bi, ti: (ti, 0))
    return pl.pallas_call(
        _mix_prompt_kernel,
        grid=(b, t // tq),
        in_specs=[
            pl.BlockSpec((None, tq, D), tile),
            _resident((1, D)),
            _resident((D, IN_COLS)),
            _resident((CONV_W, D)),
            _resident((1, D)),
            _resident((1, D)),
            _resident((1, D)),
            _resident((D, D)),
            pl.BlockSpec(memory_space=pltpu.SMEM),
            _resident((D, D)),
            _resident((D, D)),
            _resident((1, D)),
            tab, tab, tab,
        ],
        out_specs=[
            pl.BlockSpec((None, tq, D), tile),
            pl.BlockSpec((None, HALO, D), per_seq),
            pl.BlockSpec((None, WINDOW, KV_W), per_seq),
            pl.BlockSpec((None, WINDOW, KV_W), per_seq),
        ],
        out_shape=[
            jax.ShapeDtypeStruct((b, t, D), F32),
            jax.ShapeDtypeStruct((b, HALO, D), F32),
            jax.ShapeDtypeStruct((b, WINDOW, KV_W), F32),
            jax.ShapeDtypeStruct((b, WINDOW, KV_W), F32),
        ],
        scratch_shapes=[
            pltpu.VMEM((D // LANES, 32 + tq, LANES), F32),
            pltpu.VMEM((D // LANES, tq, LANES), F32),
            pltpu.VMEM((tq, D), BF16),
            pltpu.VMEM((N_KV, WINDOW + tq, LANES), BF16),
            pltpu.VMEM((N_KV, WINDOW + tq, LANES), BF16),
            pltpu.VMEM((tq, 2 * D), F32),
            pltpu.VMEM((tq, D), BF16),
        ],
        compiler_params=pltpu.CompilerParams(
            dimension_semantics=("arbitrary", "arbitrary"), vmem_limit_bytes=VMEM_LIMIT),
        name="mix_prompt",
    )(x, p["mix_pre_g"], p["w_in"], p["conv_dw_w"], p["conv_dw_b"], p["conv_ln_g"], p["conv_ln_b"],
      p["w_conv_out"], p["attn_sinks"], p["w_attn_out"], p["w_out"], p["mix_post_g"], *tables)


def _mix_sample_kernel(x_ref, st_ref, kt_ref, vt_ref, pre_ref, win_ref, cw_ref, cb_ref, lng_ref, lnb_ref,
                       wco_ref, sinkcol_ref, wao_ref, wo_ref, post_ref, cos_ref, slo_ref, shi_ref,
                       y_ref, cst_ref, kto_ref, vto_ref,
                       ubuf, cbuf, qf, qm, sbuf, pbuf, obuf, of):
    nt = x_ref.shape[0]
    s_new = SUBLANES
    sb = nt // s_new
    l_buf = kt_ref.shape[2]
    n_slab = D // LANES

    x = x_ref[...]
    h = _rmsnorm(x, pre_ref[...]).astype(BF16)

    u = _dot(h, win_ref[:, 0:D]) * jax.nn.sigmoid(_dot(h, win_ref[:, O_GLU_G:O_GLU_G + D]))
    for lg in range(n_slab):
        ubuf[lg] = u[:, lg * LANES:(lg + 1) * LANES]

    def conv_row(i, lg):
        if i < HALO:
            return st_ref[i, :, lg * LANES:(lg + 1) * LANES]
        return ubuf[lg, pl.ds(i - HALO, sb, stride=s_new), :]

    for lg in range(n_slab):
        lanes = slice(lg * LANES, (lg + 1) * LANES)
        w = [jnp.broadcast_to(cw_ref[j:j + 1, lanes], (sb, LANES)) for j in range(CONV_W)]
        acc = [jnp.broadcast_to(cb_ref[:, lanes], (sb, LANES))] * s_new
        for i in range(HALO + s_new):
            row_i = conv_row(i, lg)
            for tok in range(s_new):
                if 0 <= i - tok < CONV_W:
                    acc[tok] = acc[tok] + w[i - tok] * row_i
            if i >= s_new:
                cst_ref[i - s_new, :, lanes] = row_i
        for tok in range(s_new):
            cbuf[lg, pl.ds(tok, sb, stride=s_new), :] = acc[tok]
    c = jnp.concatenate([cbuf[lg] for lg in range(n_slab)], axis=-1)
    cs = _ln_silu(c, lng_ref[...], lnb_ref[...]).astype(BF16)
    conv_out = _dot(cs, wco_ref[...])

    cos, slo, shi = cos_ref[...], slo_ref[...], shi_ref[...]
    zq = _dot(h, win_ref[:, O_Q:O_Q + D])
    for g, rq in enumerate(_rope(zq, cos, slo, shi)):
        qf[:, g * LANES:(g + 1) * LANES] = rq * (HEAD_DIM ** -0.5)
    rk = jnp.concatenate(_rope(_dot(h, win_ref[:, O_K:O_K + KV_W]), cos, slo, shi), axis=-1)
    zv = _dot(h, win_ref[:, O_V:O_V + KV_W])
    k_new, v_new = rk.astype(BF16), zv.astype(BF16)
    k_new_t, v_new_t = rk.T, zv.T

    n_rows = N_HEADS * s_new
    nt_dims = (((1,), (1,)), ((), ()))
    low_half = lax.broadcasted_iota(jnp.int32, (nt, LANES), 1) < HEAD_DIM
    zeros = jnp.zeros((sb, s_new, LANES), F32)

    for head in range(N_HEADS):
        kv = head // GROUP
        src = qf[:, (head // 2) * LANES:(head // 2 + 1) * LANES]
        if head % 2 != kv % 2:
            src = pltpu.roll(src, HEAD_DIM, 1)
        piece = jnp.where(low_half if kv % 2 == 0 else ~low_half, src, 0.0).reshape(sb, s_new, LANES)
        qm[:, head * s_new:(head + 1) * s_new, :] = jnp.concatenate(
            [piece, zeros] if kv // 2 == 0 else [zeros, piece], axis=-1)

    keep_old = lax.broadcasted_iota(jnp.int32, (KV_W, LANES), 1) < l_buf - s_new
    shift = l_buf - s_new

    def new_columns(z_t, b):
        amount = (shift - b * s_new) % LANES
        return pltpu.roll(z_t, amount, 1) if amount else z_t

    for b in range(sb):
        qb = qm[b].astype(BF16)
        kt = kt_ref[b]
        sbuf[b] = jnp.concatenate([_dot(qb, kt.astype(BF16)),
                                   lax.dot_general(qb, k_new, nt_dims, preferred_element_type=F32)], axis=-1)
        kto_ref[b] = jnp.where(keep_old, pltpu.roll(kt, shift, 1), new_columns(k_new_t, b))

    shape3 = (sb, n_rows, l_buf + nt)
    seq = lax.broadcasted_iota(jnp.int32, shape3, 0)
    tok = jnp.bitwise_and(lax.broadcasted_iota(jnp.int32, shape3, 1), s_new - 1)
    col = lax.broadcasted_iota(jnp.int32, shape3, 2)
    first_new = l_buf + seq * s_new
    mask = ((col < l_buf) & (col > tok + l_buf - WINDOW)) | ((col >= first_new) & (col <= first_new + tok))
    s = jnp.where(mask, sbuf[...], NEG_INF)
    sink = sinkcol_ref[...].reshape(1, n_rows, 1)
    m = jnp.maximum(jnp.max(s, axis=-1, keepdims=True), sink)
    p = jnp.exp(s - m)
    den = jnp.sum(p, axis=-1, keepdims=True) + jnp.exp(sink - m)
    pbuf[...] = (p * (1.0 / den)).astype(BF16)

    for b in range(sb):
        vt = vt_ref[b]
        pb = pbuf[b]
        obuf[b] = (lax.dot_general(pb[:, 0:l_buf], vt.astype(BF16), nt_dims, preferred_element_type=F32)
                   + _dot(pb[:, l_buf:], v_new))
        vto_ref[b] = jnp.where(keep_old, pltpu.roll(vt, shift, 1), new_columns(v_new_t, b))

    for pair in range(N_HEADS // 2):
        kv = (2 * pair) // GROUP
        halves = []
        for head in (2 * pair, 2 * pair + 1):
            blk = obuf[:, head * s_new:(head + 1) * s_new, (kv // 2) * LANES:(kv // 2 + 1) * LANES]
            blk = blk.reshape(nt, LANES)
            halves.append(pltpu.roll(blk, HEAD_DIM, 1) if head % 2 != kv % 2 else blk)
        of[:, pair * LANES:(pair + 1) * LANES] = jnp.where(low_half, halves[0], halves[1])
    attn_out = _dot(of[...].astype(BF16), wao_ref[...])

    g_conv = jax.nn.sigmoid(_dot(h, win_ref[:, O_GC:O_GC + D]))
    g_attn = jax.nn.sigmoid(_dot(h, win_ref[:, O_GA:O_GA + D]))
    y = _dot((g_conv * conv_out + g_attn * attn_out).astype(BF16), wo_ref[...])
    y_ref[...] = x + _rmsnorm(y, post_ref[...])


def _mix_sample(x2d, state_t, kt, vt, p, tables):
    nb, _, l_buf = kt.shape
    s_new = x2d.shape[0] // nb
    sb = SAMPLE_SB
    nt = sb * s_new
    assert s_new == SUBLANES and l_buf == LANES and nt == LANES and nb % sb == 0
    sink_col = jnp.repeat(p["attn_sinks"], s_new).reshape(N_HEADS * s_new, 1)
    rows = lambda i: (i, 0)
    seqs = lambda i: (i, 0, 0)
    taps = lambda i: (0, i, 0)
    tab = pl.BlockSpec((nt, LANES), lambda i: (0, 0), pipeline_mode=pl.Buffered(1))
    return pl.pallas_call(
        _mix_sample_kernel,
        grid=(nb // sb,),
        in_specs=[
            pl.BlockSpec((nt, D), rows),
            pl.BlockSpec((HALO, sb, D), taps),
            pl.BlockSpec((sb, KV_W, l_buf), seqs),
            pl.BlockSpec((sb, KV_W, l_buf), seqs),
            _resident((1, D)),
            _resident((D, IN_COLS)),
            _resident((CONV_W, D)),
            _resident((1, D)),
            _resident((1, D)),
            _resident((1, D)),
            _resident((D, D)),
            _resident((N_HEADS * s_new, 1)),
            _resident((D, D)),
            _resident((D, D)),
            _resident((1, D)),
            tab, tab, tab,
        ],
        out_specs=[
            pl.BlockSpec((nt, D), rows),
            pl.BlockSpec((HALO, sb, D), taps),
            pl.BlockSpec((sb, KV_W, l_buf), seqs),
            pl.BlockSpec((sb, KV_W, l_buf), seqs),
        ],
        out_shape=[
            jax.ShapeDtypeStruct(x2d.shape, F32),
            jax.ShapeDtypeStruct((HALO, nb, D), F32),
            jax.ShapeDtypeStruct((nb, KV_W, l_buf), F32),
            jax.ShapeDtypeStruct((nb, KV_W, l_buf), F32),
        ],
        scratch_shapes=[
            pltpu.VMEM((D // LANES, nt, LANES), F32),
            pltpu.VMEM((D // LANES, nt, LANES), F32),
            pltpu.VMEM((nt, D), F32),
            pltpu.VMEM((sb, N_HEADS * s_new, KV_W), F32),
            pltpu.VMEM((sb, N_HEADS * s_new, l_buf + nt), F32),
            pltpu.VMEM((sb, N_HEADS * s_new, l_buf + nt), BF16),
            pltpu.VMEM((sb, N_HEADS * s_new, KV_W), F32),
            pltpu.VMEM((nt, D), F32),
        ],
        compiler_params=pltpu.CompilerParams(
            dimension_semantics=("arbitrary",), vmem_limit_bytes=VMEM_LIMIT),
        name="mix_sample",
    )(x2d, state_t, kt, vt, p["mix_pre_g"], p["w_in"], p["conv_dw_w"], p["conv_dw_b"], p["conv_ln_g"],
      p["conv_ln_b"], p["w_conv_out"], sink_col, p["w_attn_out"], p["w_out"], p["mix_post_g"], *tables)


def kernel(x_prompt, x_sample, state_conv, cache_k_win, cache_v_win, ffn1_pre_g, ffn1_w_up, ffn1_w_down, ffn1_post_g, mix_pre_g, w_in, conv_dw_w, conv_dw_b, conv_ln_g, conv_ln_b, w_conv_out, attn_sinks, w_attn_out, w_out, mix_post_g, ffn2_pre_g, ffn2_w_up, ffn2_w_down, ffn2_post_g):
    depth = w_in.shape[0]
    b, t, _ = x_prompt.shape
    nb, s_new, _ = x_sample.shape
    l_buf = cache_k_win.shape[2]
    matmul_weights = dict(ffn1_w_up=ffn1_w_up, ffn1_w_down=ffn1_w_down, w_in=w_in, w_conv_out=w_conv_out,
                          w_attn_out=w_attn_out, w_out=w_out, ffn2_w_up=ffn2_w_up, ffn2_w_down=ffn2_w_down)
    other = dict(ffn1_pre_g=ffn1_pre_g, ffn1_post_g=ffn1_post_g, mix_pre_g=mix_pre_g, conv_dw_w=conv_dw_w,
                 conv_dw_b=conv_dw_b, conv_ln_g=conv_ln_g, conv_ln_b=conv_ln_b, attn_sinks=attn_sinks,
                 mix_post_g=mix_post_g, ffn2_pre_g=ffn2_pre_g, ffn2_post_g=ffn2_post_g)
    tab_p = _rope_tables(jnp.arange(t))
    tab_s = _rope_tables(jnp.tile(PAST_LEN + jnp.arange(s_new), SAMPLE_SB))

    hp = x_prompt.reshape(b * t, D)
    hs = x_sample.reshape(nb * s_new, D)
    outs = [[] for _ in range(6)]
    for l in range(depth):
        p = {n: w[l].astype(BF16) for n, w in matmul_weights.items()}
        for n, w in other.items():
            p[n] = w[l] if w[l].ndim == 2 or n == "attn_sinks" else w[l].reshape(1, -1)

        hp = _ffn(hp, p["ffn1_pre_g"], p["ffn1_w_up"], p["ffn1_w_down"], p["ffn1_post_g"])
        hp, c1, k1, v1 = _mix_prompt(hp.reshape(b, t, D), p, tab_p)
        hp = _ffn(hp.reshape(b * t, D), p["ffn2_pre_g"], p["ffn2_w_up"], p["ffn2_w_down"], p["ffn2_post_g"])

        rows_last = lambda w: jnp.transpose(w, (0, 2, 3, 1)).reshape(nb, KV_W, l_buf)
        rows_back = lambda w: jnp.transpose(w.reshape(nb, N_KV, HEAD_DIM, l_buf), (0, 3, 1, 2))
        hs = _ffn(hs, p["ffn1_pre_g"], p["ffn1_w_up"], p["ffn1_w_down"], p["ffn1_post_g"])
        hs, c2, k2, v2 = _mix_sample(hs, jnp.transpose(state_conv[l], (1, 0, 2)),
                                     rows_last(cache_k_win[l]), rows_last(cache_v_win[l]), p, tab_s)
        hs = _ffn(hs, p["ffn2_pre_g"], p["ffn2_w_up"], p["ffn2_w_down"], p["ffn2_post_g"])

        kv_shape = lambda n: (n, -1, N_KV, HEAD_DIM)
        for acc, val in zip(outs, (c1, k1.reshape(kv_shape(b)), v1.reshape(kv_shape(b)),
                                   jnp.transpose(c2, (1, 0, 2)), rows_back(k2), rows_back(v2))):
            acc.append(val)
    return (hp.reshape(b, t, D), hs.reshape(nb, s_new, D), *(jnp.stack(o) for o in outs))
```

```python
import math

import jax
import jax.numpy as jnp
from jax import lax
from jax.experimental import pallas as pl
from jax.experimental.pallas import tpu as pltpu

D = 1024
N_HEADS = 16
HEAD_DIM = 64
N_KV = 4
GROUP = 4
ROT_DIM = 16
ROPE_THETA = 500000.0
WINDOW = 128
CONV_W = 31
HALO = CONV_W - 1
D_FF = 2816
KV_W = N_KV * HEAD_DIM
EPS = 1e-6
NEG_INF = -1e30
PAST_LEN = 8192
O_GLU_G = D
O_Q = 2 * D
O_K = O_Q + D
O_V = O_K + KV_W
O_GC = O_V + KV_W
O_GA = O_GC + D
IN_COLS = O_GA + D

LANES = 128
SUBLANES = 8
VMEM_LIMIT = 56 * 1024 * 1024

FFN_TM = 1024
MIX_TQ = 256
CONV_STRIDE = 4
SAMPLE_SB = 16

F32 = jnp.float32
BF16 = jnp.bfloat16


def _dot(a, b):
    return jnp.dot(a, b, preferred_element_type=F32)


def _rmsnorm(x, g):
    r = lax.rsqrt(jnp.mean(x * x, axis=-1, keepdims=True) + EPS)
    return x * r * g


def _resident(shape):
    return pl.BlockSpec(shape, lambda *_: (0,) * len(shape), pipeline_mode=pl.Buffered(1))


def _ffn_kernel(x_ref, pre_ref, wup_ref, wdn_ref, post_ref, o_ref):
    x = x_ref[...]
    h = _rmsnorm(x, pre_ref[...]).astype(BF16)
    gu = _dot(h, wup_ref[...])
    a = (jax.nn.silu(gu[:, 0:D_FF]) * gu[:, D_FF:]).astype(BF16)
    y = _dot(a, wdn_ref[...])
    o_ref[...] = x + 0.5 * _rmsnorm(y, post_ref[...])


def _ffn(x2d, pre_g, w_up, w_down, post_g):
    n = x2d.shape[0]
    tm = min(FFN_TM, n)
    assert n % tm == 0 and D_FF % LANES == 0
    return pl.pallas_call(
        _ffn_kernel,
        grid=(n // tm,),
        in_specs=[
            pl.BlockSpec((tm, D), lambda i: (i, 0)),
            _resident((1, D)),
            _resident((D, 2 * D_FF)),
            _resident((D_FF, D)),
            _resident((1, D)),
        ],
        out_specs=pl.BlockSpec((tm, D), lambda i: (i, 0)),
        out_shape=jax.ShapeDtypeStruct((n, D), F32),
        compiler_params=pltpu.CompilerParams(
            dimension_semantics=("arbitrary",), vmem_limit_bytes=VMEM_LIMIT),
        name="ffn",
    )(x2d, pre_g, w_up, w_down, post_g)


def _rope(z, cos, sin_lo, sin_hi):
    half = ROT_DIM // 2
    outs = []
    for g in range(z.shape[-1] // LANES):
        zg = z[:, g * LANES:(g + 1) * LANES]
        outs.append(zg * cos
                    + pltpu.roll(zg, LANES - half, 1) * sin_lo
                    + pltpu.roll(zg, half, 1) * sin_hi)
    return outs


def _rope_tables(pos):
    half = ROT_DIM // 2
    inv = jnp.exp(-math.log(ROPE_THETA) * jnp.arange(0, ROT_DIM, 2, dtype=F32) / ROT_DIM)
    ang = pos.astype(F32)[:, None] * inv[None, :]
    cos, sin = jnp.cos(ang), jnp.sin(ang)
    n = pos.shape[0]
    pad = jnp.zeros((n, HEAD_DIM - ROT_DIM), F32)
    zero = jnp.zeros((n, half), F32)
    cos_h = jnp.concatenate([cos, cos, pad + 1.0], axis=1)
    lo_h = jnp.concatenate([-sin, zero, pad], axis=1)
    hi_h = jnp.concatenate([zero, sin, pad], axis=1)
    rep = LANES // HEAD_DIM
    return tuple(jnp.tile(t, (1, rep)) for t in (cos_h, lo_h, hi_h))


def _ln_silu(c, g, b):
    mu = jnp.mean(c, axis=-1, keepdims=True)
    var = jnp.mean(jnp.square(c - mu), axis=-1, keepdims=True)
    return jax.nn.silu((c - mu) * lax.rsqrt(var + EPS) * g + b)


def _dup_halves(z, kv):
    src = z[:, (kv // 2) * LANES:(kv // 2 + 1) * LANES]
    other = pltpu.roll(src, HEAD_DIM, 1)
    low = lax.broadcasted_iota(jnp.int32, src.shape, 1) < HEAD_DIM
    return jnp.where(low, src, other) if kv % 2 == 0 else jnp.where(low, other, src)


def _mix_prompt_kernel(x_ref, pre_ref, win_ref, cw_ref, cb_ref, lng_ref, lnb_ref, wco_ref, sinks_ref,
                       wao_ref, wo_ref, post_ref, cos_ref, slo_ref, shi_ref,
                       y_ref, cst_ref, kw_ref, vw_ref,
                       ubuf, cbuf, qbuf, kdup, vdup, gbuf, obuf):
    tq = x_ref.shape[0]
    t = pl.program_id(1)
    n_slab = D // LANES

    @pl.when(t == 0)
    def _():
        ubuf[:, 0:32, :] = jnp.zeros((n_slab, 32, LANES), F32)
        kdup[:, 0:WINDOW, :] = jnp.zeros((N_KV, WINDOW, LANES), BF16)
        vdup[:, 0:WINDOW, :] = jnp.zeros((N_KV, WINDOW, LANES), BF16)

    @pl.when(t > 0)
    def _():
        ubuf[:, 0:32, :] = ubuf[:, tq:tq + 32, :]
        kdup[:, 0:WINDOW, :] = kdup[:, tq:tq + WINDOW, :]
        vdup[:, 0:WINDOW, :] = vdup[:, tq:tq + WINDOW, :]

    x = x_ref[...]
    h = _rmsnorm(x, pre_ref[...]).astype(BF16)

    u = _dot(h, win_ref[:, 0:D]) * jax.nn.sigmoid(_dot(h, win_ref[:, O_GLU_G:O_GLU_G + D]))
    for lg in range(n_slab):
        ubuf[lg, 32:32 + tq, :] = u[:, lg * LANES:(lg + 1) * LANES]
    cst_ref[...] = u[tq - HALO:, :]

    rows_per_unit = CONV_STRIDE * SUBLANES
    for lg in range(n_slab):
        lanes = slice(lg * LANES, (lg + 1) * LANES)
        for base in range(0, tq, rows_per_unit):
            w = {}
            acc = [jnp.broadcast_to(cb_ref[:, lanes], (SUBLANES, LANES))] * CONV_STRIDE
            for off in range(CONV_STRIDE - 1 + CONV_W):
                win = ubuf[lg, pl.ds(base + off + 2, SUBLANES, stride=CONV_STRIDE), :]
                if off < CONV_W:
                    w[off] = jnp.broadcast_to(cw_ref[off:off + 1, lanes], (SUBLANES, LANES))
                for r in range(CONV_STRIDE):
                    if 0 <= off - r < CONV_W:
                        acc[r] = acc[r] + w[off - r] * win
            for r in range(CONV_STRIDE):
                cbuf[lg, pl.ds(base + r, SUBLANES, stride=CONV_STRIDE), :] = acc[r]

    cos, slo, shi = cos_ref[...], slo_ref[...], shi_ref[...]
    chunk = 4 * LANES

    def project(col):
        return _dot(h, win_ref[:, col:col + chunk])

    def kv_chunk(z):
        rk = jnp.concatenate(_rope(z[:, 0:KV_W], cos, slo, shi), axis=-1)
        zv = z[:, KV_W:2 * KV_W]
        for kv in range(N_KV):
            kdup[kv, WINDOW:WINDOW + tq, :] = _dup_halves(rk, kv).astype(BF16)
            vdup[kv, WINDOW:WINDOW + tq, :] = _dup_halves(zv, kv).astype(BF16)
        kw_ref[...] = rk[tq - WINDOW:, :]
        vw_ref[...] = zv[tq - WINDOW:, :]

    def q_chunk(c, z):
        for g, rq in enumerate(_rope(z, cos, slo, shi)):
            col = c * chunk + g * LANES
            qbuf[:, col:col + LANES] = (rq * (HEAD_DIM ** -0.5)).astype(BF16)

    def gate_chunk(c, z):
        gbuf[:, c * chunk:(c + 1) * chunk] = jax.nn.sigmoid(z)

    kv_chunk(project(O_K))
    for c in range(D // chunk):
        q_chunk(c, project(O_Q + c * chunk))
    for c in range(2 * D // chunk):
        gate_chunk(c, project(O_GC + c * chunk))

    c = jnp.concatenate([cbuf[lg] for lg in range(n_slab)], axis=-1)
    cs = _ln_silu(c, lng_ref[...], lnb_ref[...]).astype(BF16)
    conv_out = _dot(cs, wco_ref[...])

    row = lax.broadcasted_iota(jnp.int32, (WINDOW, 2 * WINDOW), 0)
    col = lax.broadcasted_iota(jnp.int32, (WINDOW, 2 * WINDOW), 1)
    band = (col > row) & (col <= row + WINDOW)
    low_q = lax.broadcasted_iota(jnp.int32, (WINDOW, LANES), 1) < HEAD_DIM
    low_kv = lax.broadcasted_iota(jnp.int32, (2 * WINDOW, LANES), 1) < HEAD_DIM
    zero_q = jnp.zeros((WINDOW, LANES), BF16)
    zero_kv = jnp.zeros((2 * WINDOW, LANES), BF16)
    sink_slot = lax.broadcasted_iota(jnp.int32, (WINDOW, LANES), 1) == 0
    not_key0 = lax.broadcasted_iota(jnp.int32, (2 * WINDOW, LANES), 0) > 0
    no_key = jnp.full((WINDOW, LANES), NEG_INF, F32)
    ones_bd = jnp.concatenate([low_kv, ~low_kv], axis=0).astype(F32).astype(BF16)
    for i in range(tq // WINDOW):
        r0 = i * WINDOW
        mask = band if i > 0 else band & (col >= jnp.where(t > 0, 0, WINDOW))
        for kv in range(N_KV):
            kd = kdup[kv, r0:r0 + 2 * WINDOW, :]
            vd = vdup[kv, r0:r0 + 2 * WINDOW, :]
            v_bd = jnp.concatenate([jnp.where(low_kv & not_key0, vd, zero_kv),
                                    jnp.where(~low_kv & not_key0, vd, zero_kv)], axis=0)
            v_aug = jnp.concatenate([v_bd, ones_bd], axis=-1)
            q_rows = []
            for pair in range(GROUP // 2):
                tile_col = (kv * (GROUP // 2) + pair) * LANES
                qp = qbuf[r0:r0 + WINDOW, tile_col:tile_col + LANES]
                q_rows += [jnp.where(low_q, qp, zero_q), jnp.where(low_q, zero_q, qp)]
            s_all = lax.dot_general(jnp.concatenate(q_rows, axis=0), kd, (((1,), (1,)), ((), ())),
                                    preferred_element_type=F32)
            probs = []
            for g in range(GROUP):
                fill = jnp.concatenate([jnp.where(sink_slot, sinks_ref[kv * GROUP + g], no_key), no_key], axis=-1)
                s = jnp.where(mask, s_all[g * WINDOW:(g + 1) * WINDOW], fill)
                probs.append(jnp.exp(s - jnp.max(s, axis=-1, keepdims=True)).astype(BF16))
            for pair in range(GROUP // 2):
                tile_col = (kv * (GROUP // 2) + pair) * LANES
                oa = _dot(jnp.concatenate(probs[2 * pair:2 * pair + 2], axis=-1), v_aug)
                obuf[r0:r0 + WINDOW, tile_col:tile_col + LANES] = (oa[:, 0:LANES] / oa[:, LANES:]).astype(BF16)
    attn_out = _dot(obuf[...], wao_ref[...])

    merged = gbuf[:, 0:D] * conv_out + gbuf[:, D:2 * D] * attn_out
    y = _dot(merged.astype(BF16), wo_ref[...])
    y_ref[...] = x + _rmsnorm(y, post_ref[...])


def _mix_prompt(x, p, tables):
    b, t, _ = x.shape
    tq = MIX_TQ
    assert t % tq == 0 and tq % WINDOW == 0 and tq % (CONV_STRIDE * SUBLANES) == 0
    tile = lambda bi, ti: (bi, ti, 0)
    per_seq = lambda bi, ti: (bi, 0, 0)
    tab = pl.BlockSpec((tq, LANES), lambda bi, ti: (ti, 0))
    return pl.pallas_call(
        _mix_prompt_kernel,
        grid=(b, t // tq),
        in_specs=[
            pl.BlockSpec((None, tq, D), tile),
            _resident((1, D)),
            _resident((D, IN_COLS)),
            _resident((CONV_W, D)),
            _resident((1, D)),
            _resident((1, D)),
            _resident((1, D)),
            _resident((D, D)),
            pl.BlockSpec(memory_space=pltpu.SMEM),
            _resident((D, D)),
            _resident((D, D)),
            _resident((1, D)),
            tab, tab, tab,
        ],
        out_specs=[
            pl.BlockSpec((None, tq, D), tile),
            pl.BlockSpec((None, HALO, D), per_seq),
            pl.BlockSpec((None, WINDOW, KV_W), per_seq),
            pl.BlockSpec((None, WINDOW, KV_W), per_seq),
        ],
        out_shape=[
            jax.ShapeDtypeStruct((b, t, D), F32),
            jax.ShapeDtypeStruct((b, HALO, D), F32),
            jax.ShapeDtypeStruct((b, WINDOW, KV_W), F32),
            jax.ShapeDtypeStruct((b, WINDOW, KV_W), F32),
        ],
        scratch_shapes=[
            pltpu.VMEM((D // LANES, 32 + tq, LANES), F32),
            pltpu.VMEM((D // LANES, tq, LANES), F32),
            pltpu.VMEM((tq, D), BF16),
            pltpu.VMEM((N_KV, WINDOW + tq, LANES), BF16),
            pltpu.VMEM((N_KV, WINDOW + tq, LANES), BF16),
            pltpu.VMEM((tq, 2 * D), F32),
            pltpu.VMEM((tq, D), BF16),
        ],
        compiler_params=pltpu.CompilerParams(
            dimension_semantics=("arbitrary", "arbitrary"), vmem_limit_bytes=VMEM_LIMIT),
        name="mix_prompt",
    )(x, p["mix_pre_g"], p["w_in"], p["conv_dw_w"], p["conv_dw_b"], p["conv_ln_g"], p["conv_ln_b"],
      p["w_conv_out"], p["attn_sinks"], p["w_attn_out"], p["w_out"], p["mix_post_g"], *tables)


def _mix_sample_kernel(x_ref, st_ref, kt_ref, vt_ref, pre_ref, win_ref, cw_ref, cb_ref, lng_ref, lnb_ref,
                       wco_ref, sinkcol_ref, wao_ref, wo_ref, post_ref, cos_ref, slo_ref, shi_ref,
                       y_ref, cst_ref, kto_ref, vto_ref,
                       ubuf, cbuf, qf, qm, sbuf, pbuf, obuf, of):
    nt = x_ref.shape[0]
    s_new = SUBLANES
    sb = nt // s_new
    l_buf = kt_ref.shape[2]
    n_slab = D // LANES

    x = x_ref[...]
    h = _rmsnorm(x, pre_ref[...]).astype(BF16)

    u = _dot(h, win_ref[:, 0:D]) * jax.nn.sigmoid(_dot(h, win_ref[:, O_GLU_G:O_GLU_G + D]))
    for lg in range(n_slab):
        ubuf[lg] = u[:, lg * LANES:(lg + 1) * LANES]

    def conv_row(i, lg):
        if i < HALO:
            return st_ref[i, :, lg * LANES:(lg + 1) * LANES]
        return ubuf[lg, pl.ds(i - HALO, sb, stride=s_new), :]

    for lg in range(n_slab):
        lanes = slice(lg * LANES, (lg + 1) * LANES)
        w = [jnp.broadcast_to(cw_ref[j:j + 1, lanes], (sb, LANES)) for j in range(CONV_W)]
        acc = [jnp.broadcast_to(cb_ref[:, lanes], (sb, LANES))] * s_new
        for i in range(HALO + s_new):
            row_i = conv_row(i, lg)
            for tok in range(s_new):
                if 0 <= i - tok < CONV_W:
                    acc[tok] = acc[tok] + w[i - tok] * row_i
            if i >= s_new:
                cst_ref[i - s_new, :, lanes] = row_i
        for tok in range(s_new):
            cbuf[lg, pl.ds(tok, sb, stride=s_new), :] = acc[tok]
    c = jnp.concatenate([cbuf[lg] for lg in range(n_slab)], axis=-1)
    cs = _ln_silu(c, lng_ref[...], lnb_ref[...]).astype(BF16)
    conv_out = _dot(cs, wco_ref[...])

    cos, slo, shi = cos_ref[...], slo_ref[...], shi_ref[...]
    zq = _dot(h, win_ref[:, O_Q:O_Q + D])
    for g, rq in enumerate(_rope(zq, cos, slo, shi)):
        qf[:, g * LANES:(g + 1) * LANES] = rq * (HEAD_DIM ** -0.5)
    rk = jnp.concatenate(_rope(_dot(h, win_ref[:, O_K:O_K + KV_W]), cos, slo, shi), axis=-1)
    zv = _dot(h, win_ref[:, O_V:O_V + KV_W])
    k_new, v_new = rk.astype(BF16), zv.astype(BF16)
    k_new_t, v_new_t = rk.T, zv.T

    n_rows = N_HEADS * s_new
    nt_dims = (((1,), (1,)), ((), ()))
    low_half = lax.broadcasted_iota(jnp.int32, (nt, LANES), 1) < HEAD_DIM
    zeros = jnp.zeros((sb, s_new, LANES), F32)

    for head in range(N_HEADS):
        kv = head // GROUP
        src = qf[:, (head // 2) * LANES:(head // 2 + 1) * LANES]
        if head % 2 != kv % 2:
            src = pltpu.roll(src, HEAD_DIM, 1)
        piece = jnp.where(low_half if kv % 2 == 0 else ~low_half, src, 0.0).reshape(sb, s_new, LANES)
        qm[:, head * s_new:(head + 1) * s_new, :] = jnp.concatenate(
            [piece, zeros] if kv // 2 == 0 else [zeros, piece], axis=-1)

    keep_old = lax.broadcasted_iota(jnp.int32, (KV_W, LANES), 1) < l_buf - s_new
    shift = l_buf - s_new

    def new_columns(z_t, b):
        amount = (shift - b * s_new) % LANES
        return pltpu.roll(z_t, amount, 1) if amount else z_t

    for b in range(sb):
        qb = qm[b].astype(BF16)
        kt = kt_ref[b]
        sbuf[b] = jnp.concatenate([_dot(qb, kt.astype(BF16)),
                                   lax.dot_general(qb, k_new, nt_dims, preferred_element_type=F32)], axis=-1)
        kto_ref[b] = jnp.where(keep_old, pltpu.roll(kt, shift, 1), new_columns(k_new_t, b))

    shape3 = (sb, n_rows, l_buf + nt)
    seq = lax.broadcasted_iota(jnp.int32, shape3, 0)
    tok = jnp.bitwise_and(lax.broadcasted_iota(jnp.int32, shape3, 1), s_new - 1)
    col = lax.broadcasted_iota(jnp.int32, shape3, 2)
    first_new = l_buf + seq * s_new
    mask = ((col < l_buf) & (col > tok + l_buf - WINDOW)) | ((col >= first_new) & (col <= first_new + tok))
    s = jnp.where(mask, sbuf[...], NEG_INF)
    sink = sinkcol_ref[...].reshape(1, n_rows, 1)
    m = jnp.maximum(jnp.max(s, axis=-1, keepdims=True), sink)
    p = jnp.exp(s - m)
    den = jnp.sum(p, axis=-1, keepdims=True) + jnp.exp(sink - m)
    pbuf[...] = (p * (1.0 / den)).astype(BF16)

    for b in range(sb):
        vt = vt_ref[b]
        pb = pbuf[b]
        obuf[b] = (lax.dot_general(pb[:, 0:l_buf], vt.astype(BF16), nt_dims, preferred_element_type=F32)
                   + _dot(pb[:, l_buf:], v_new))
        vto_ref[b] = jnp.where(keep_old, pltpu.roll(vt, shift, 1), new_columns(v_new_t, b))

    for pair in range(N_HEADS // 2):
        kv = (2 * pair) // GROUP
        halves = []
        for head in (2 * pair, 2 * pair + 1):
            blk = obuf[:, head * s_new:(head + 1) * s_new, (kv // 2) * LANES:(kv // 2 + 1) * LANES]
            blk = blk.reshape(nt, LANES)
            halves.append(pltpu.roll(blk, HEAD_DIM, 1) if head % 2 != kv % 2 else blk)
        of[:, pair * LANES:(pair + 1) * LANES] = jnp.where(low_half, halves[0], halves[1])
    attn_out = _dot(of[...].astype(BF16), wao_ref[...])

    g_conv = jax.nn.sigmoid(_dot(h, win_ref[:, O_GC:O_GC + D]))
    g_attn = jax.nn.sigmoid(_dot(h, win_ref[:, O_GA:O_GA + D]))
    y = _dot((g_conv * conv_out + g_attn * attn_out).astype(BF16), wo_ref[...])
    y_ref[...] = x + _rmsnorm(y, post_ref[...])


def _mix_sample(x2d, state_t, kt, vt, p, tables):
    nb, _, l_buf = kt.shape
    s_new = x2d.shape[0] // nb
    sb = SAMPLE_SB
    nt = sb * s_new
    assert s_new == SUBLANES and l_buf == LANES and nt == LANES and nb % sb == 0
    sink_col = jnp.repeat(p["attn_sinks"], s_new).reshape(N_HEADS * s_new, 1)
    rows = lambda i: (i, 0)
    seqs = lambda i: (i, 0, 0)
    taps = lambda i: (0, i, 0)
    tab = pl.BlockSpec((nt, LANES), lambda i: (0, 0), pipeline_mode=pl.Buffered(1))
    return pl.pallas_call(
        _mix_sample_kernel,
        grid=(nb // sb,),
        in_specs=[
            pl.BlockSpec((nt, D), rows),
            pl.BlockSpec((HALO, sb, D), taps),
            pl.BlockSpec((sb, KV_W, l_buf), seqs),
            pl.BlockSpec((sb, KV_W, l_buf), seqs),
            _resident((1, D)),
            _resident((D, IN_COLS)),
            _resident((CONV_W, D)),
            _resident((1, D)),
            _resident((1, D)),
            _resident((1, D)),
            _resident((D, D)),
            _resident((N_HEADS * s_new, 1)),
            _resident((D, D)),
            _resident((D, D)),
            _resident((1, D)),
            tab, tab, tab,
        ],
        out_specs=[
            pl.BlockSpec((nt, D), rows),
            pl.BlockSpec((HALO, sb, D), taps),
            pl.BlockSpec((sb, KV_W, l_buf), seqs),
            pl.BlockSpec((sb, KV_W, l_buf), seqs),
        ],
        out_shape=[
            jax.ShapeDtypeStruct(x2d.shape, F32),
            jax.ShapeDtypeStruct((HALO, nb, D), F32),
            jax.ShapeDtypeStruct((nb, KV_W, l_buf), F32),
            jax.ShapeDtypeStruct((nb, KV_W, l_buf), F32),
        ],
        scratch_shapes=[
            pltpu.VMEM((D // LANES, nt, LANES), F32),
            pltpu.VMEM((D // LANES, nt, LANES), F32),
            pltpu.VMEM((nt, D), F32),
            pltpu.VMEM((sb, N_HEADS * s_new, KV_W), F32),
            pltpu.VMEM((sb, N_HEADS * s_new, l_buf + nt), F32),
            pltpu.VMEM((sb, N_HEADS * s_new, l_buf + nt), BF16),
            pltpu.VMEM((sb, N_HEADS * s_new, KV_W), F32),
            pltpu.VMEM((nt, D), F32),
        ],
        compiler_params=pltpu.CompilerParams(
            dimension_semantics=("arbitrary",), vmem_limit_bytes=VMEM_LIMIT),
        name="mix_sample",
    )(x2d, state_t, kt, vt, p["mix_pre_g"], p["w_in"], p["conv_dw_w"], p["conv_dw_b"], p["conv_ln_g"],
      p["conv_ln_b"], p["w_conv_out"], sink_col, p["w_attn_out"], p["w_out"], p["mix_post_g"], *tables)


def kernel(x_prompt, x_sample, state_conv, cache_k_win, cache_v_win, ffn1_pre_g, ffn1_w_up, ffn1_w_down, ffn1_post_g, mix_pre_g, w_in, conv_dw_w, conv_dw_b, conv_ln_g, conv_ln_b, w_conv_out, attn_sinks, w_attn_out, w_out, mix_post_g, ffn2_pre_g, ffn2_w_up, ffn2_w_down, ffn2_post_g):
    depth = w_in.shape[0]
    b, t, _ = x_prompt.shape
    nb, s_new, _ = x_sample.shape
    l_buf = cache_k_win.shape[2]
    matmul_weights = dict(ffn1_w_up=ffn1_w_up, ffn1_w_down=ffn1_w_down, w_in=w_in, w_conv_out=w_conv_out,
                          w_attn_out=w_attn_out, w_out=w_out, ffn2_w_up=ffn2_w_up, ffn2_w_down=ffn2_w_down)
    other = dict(ffn1_pre_g=ffn1_pre_g, ffn1_post_g=ffn1_post_g, mix_pre_g=mix_pre_g, conv_dw_w=conv_dw_w,
                 conv_dw_b=conv_dw_b, conv_ln_g=conv_ln_g, conv_ln_b=conv_ln_b, attn_sinks=attn_sinks,
                 mix_post_g=mix_post_g, ffn2_pre_g=ffn2_pre_g, ffn2_post_g=ffn2_post_g)
    tab_p = _rope_tables(jnp.arange(t))
    tab_s = _rope_tables(jnp.tile(PAST_LEN + jnp.arange(s_new), SAMPLE_SB))

    hp = x_prompt.reshape(b * t, D)
    hs = x_sample.reshape(nb * s_new, D)
    outs = [[] for _ in range(6)]
    for l in range(depth):
        p = {n: w[l].astype(BF16) for n, w in matmul_weights.items()}
        for n, w in other.items():
            p[n] = w[l] if w[l].ndim == 2 or n == "attn_sinks" else w[l].reshape(1, -1)

        hp = _ffn(hp, p["ffn1_pre_g"], p["ffn1_w_up"], p["ffn1_w_down"], p["ffn1_post_g"])
        hp, c1, k1, v1 = _mix_prompt(hp.reshape(b, t, D), p, tab_p)
        hp = _ffn(hp.reshape(b * t, D), p["ffn2_pre_g"], p["ffn2_w_up"], p["ffn2_w_down"], p["ffn2_post_g"])

        rows_last = lambda w: jnp.transpose(w, (0, 2, 3, 1)).reshape(nb, KV_W, l_buf)
        rows_back = lambda w: jnp.transpose(w.reshape(nb, N_KV, HEAD_DIM, l_buf), (0, 3, 1, 2))
        hs = _ffn(hs, p["ffn1_pre_g"], p["ffn1_w_up"], p["ffn1_w_down"], p["ffn1_post_g"])
        hs, c2, k2, v2 = _mix_sample(hs, jnp.transpose(state_conv[l], (1, 0, 2)),
                                     rows_last(cache_k_win[l]), rows_last(cache_v_win[l]), p, tab_s)
        hs = _ffn(hs, p["ffn2_pre_g"], p["ffn2_w_up"], p["ffn2_w_down"], p["ffn2_post_g"])

        kv_shape = lambda n: (n, -1, N_KV, HEAD_DIM)
        for acc, val in zip(outs, (c1, k1.reshape(kv_shape(b)), v1.reshape(kv_shape(b)),
                                   jnp.transpose(c2, (1, 0, 2)), rows_back(k2), rows_back(v2))):
            acc.append(val)
    return (hp.reshape(b, t, D), hs.reshape(nb, s_new, D), *(jnp.stack(o) for o in outs))
```

```python
import math

import jax
import jax.numpy as jnp
from jax import lax
from jax.experimental import pallas as pl
from jax.experimental.pallas import tpu as pltpu

D = 1024
N_HEADS = 16
HEAD_DIM = 64
N_KV = 4
GROUP = 4
ROT_DIM = 16
ROPE_THETA = 500000.0
WINDOW = 128
CONV_W = 31
HALO = CONV_W - 1
D_FF = 2816
KV_W = N_KV * HEAD_DIM
EPS = 1e-6
NEG_INF = -1e30
PAST_LEN = 8192
O_GLU_G = D
O_Q = 2 * D
O_K = O_Q + D
O_V = O_K + KV_W
O_GC = O_V + KV_W
O_GA = O_GC + D
IN_COLS = O_GA + D

LANES = 128
SUBLANES = 8
VMEM_LIMIT = 56 * 1024 * 1024

FFN_TM = 1024
FFN_TM_FIRST = 512
MIX_TQ = 256
CONV_STRIDE = 4
SAMPLE_SB = 16

F32 = jnp.float32
BF16 = jnp.bfloat16


def _dot(a, b):
    return jnp.dot(a, b, preferred_element_type=F32)


def _rmsnorm(x, g):
    r = lax.rsqrt(jnp.mean(x * x, axis=-1, keepdims=True) + EPS)
    return x * r * g


def _resident(shape):
    return pl.BlockSpec(shape, lambda *_: (0,) * len(shape), pipeline_mode=pl.Buffered(1))


def _ffn_kernel(x_ref, pre_ref, wup_ref, wdn_ref, post_ref, *rest):
    n_cast = len(rest) // 2
    cast_src, o_ref, cast_dst = rest[:n_cast], rest[n_cast], rest[n_cast + 1:]
    x = x_ref[...]
    h = _rmsnorm(x, pre_ref[...]).astype(BF16)
    gu = _dot(h, wup_ref[...])
    a = (jax.nn.silu(gu[:, 0:D_FF]) * gu[:, D_FF:]).astype(BF16)
    y = _dot(a, wdn_ref[...])
    o_ref[...] = x + 0.5 * _rmsnorm(y, post_ref[...])
    for src, dst in zip(cast_src, cast_dst):
        dst[...] = src[...].astype(BF16)


def _ffn(x2d, pre_g, w_up, w_down, post_g, tm, to_bf16=()):
    n = x2d.shape[0]
    steps = n // tm
    assert n % tm == 0 and D_FF % LANES == 0
    bf16_rows = 2 * SUBLANES
    tile = pl.BlockSpec((tm, D), lambda i: (i, 0))

    def slab(w):
        blocks = steps if w.shape[0] % (steps * bf16_rows) == 0 else steps // 2
        assert w.shape[0] % (blocks * bf16_rows) == 0
        return pl.BlockSpec((w.shape[0] // blocks, w.shape[1]), lambda i: (jnp.minimum(i, blocks - 1), 0))

    slabs = [slab(w) for w in to_bf16]
    y, *converted = pl.pallas_call(
        _ffn_kernel,
        grid=(steps,),
        in_specs=[
            tile,
            _resident((1, D)),
            _resident((D, 2 * D_FF)),
            _resident((D_FF, D)),
            _resident((1, D)),
            *slabs,
        ],
        out_specs=[tile, *slabs],
        out_shape=[jax.ShapeDtypeStruct((n, D), F32)] + [jax.ShapeDtypeStruct(w.shape, BF16) for w in to_bf16],
        compiler_params=pltpu.CompilerParams(
            dimension_semantics=("arbitrary",), vmem_limit_bytes=VMEM_LIMIT),
        name="ffn",
    )(x2d, pre_g, w_up, w_down, post_g, *to_bf16)
    return y, converted


def _rope(z, cos, sin_lo, sin_hi):
    half = ROT_DIM // 2
    outs = []
    for g in range(z.shape[-1] // LANES):
        zg = z[:, g * LANES:(g + 1) * LANES]
        outs.append(zg * cos
                    + pltpu.roll(zg, LANES - half, 1) * sin_lo
                    + pltpu.roll(zg, half, 1) * sin_hi)
    return outs


def _rope_tables(pos):
    half = ROT_DIM // 2
    inv = jnp.exp(-math.log(ROPE_THETA) * jnp.arange(0, ROT_DIM, 2, dtype=F32) / ROT_DIM)
    ang = pos.astype(F32)[:, None] * inv[None, :]
    cos, sin = jnp.cos(ang), jnp.sin(ang)
    n = pos.shape[0]
    pad = jnp.zeros((n, HEAD_DIM - ROT_DIM), F32)
    zero = jnp.zeros((n, half), F32)
    cos_h = jnp.concatenate([cos, cos, pad + 1.0], axis=1)
    lo_h = jnp.concatenate([-sin, zero, pad], axis=1)
    hi_h = jnp.concatenate([zero, sin, pad], axis=1)
    rep = LANES // HEAD_DIM
    return tuple(jnp.tile(t, (1, rep)) for t in (cos_h, lo_h, hi_h))


def _ln_silu(c, g, b):
    mu = jnp.mean(c, axis=-1, keepdims=True)
    var = jnp.mean(jnp.square(c - mu), axis=-1, keepdims=True)
    return jax.nn.silu((c - mu) * lax.rsqrt(var + EPS) * g + b)


def _dup_halves(z, kv):
    src = z[:, (kv // 2) * LANES:(kv // 2 + 1) * LANES]
    other = pltpu.roll(src, HEAD_DIM, 1)
    low = lax.broadcasted_iota(jnp.int32, src.shape, 1) < HEAD_DIM
    return jnp.where(low, src, other) if kv % 2 == 0 else jnp.where(low, other, src)


def _mix_prompt_kernel(x_ref, pre_ref, win_ref, cw_ref, cb_ref, lng_ref, lnb_ref, wco_ref, sinks_ref,
                       wao_ref, wo_ref, post_ref, cos_ref, slo_ref, shi_ref,
                       y_ref, cst_ref, kw_ref, vw_ref,
                       ubuf, cbuf, qbuf, kdup, vdup, gbuf, obuf):
    tq = x_ref.shape[0]
    t = pl.program_id(1)
    n_slab = D // LANES

    @pl.when(t == 0)
    def _():
        ubuf[:, 0:32, :] = jnp.zeros((n_slab, 32, LANES), F32)
        kdup[:, 0:WINDOW, :] = jnp.zeros((N_KV, WINDOW, LANES), BF16)
        vdup[:, 0:WINDOW, :] = jnp.zeros((N_KV, WINDOW, LANES), BF16)

    @pl.when(t > 0)
    def _():
        ubuf[:, 0:32, :] = ubuf[:, tq:tq + 32, :]
        kdup[:, 0:WINDOW, :] = kdup[:, tq:tq + WINDOW, :]
        vdup[:, 0:WINDOW, :] = vdup[:, tq:tq + WINDOW, :]

    x = x_ref[...]
    h = _rmsnorm(x, pre_ref[...]).astype(BF16)

    u = _dot(h, win_ref[:, 0:D]) * jax.nn.sigmoid(_dot(h, win_ref[:, O_GLU_G:O_GLU_G + D]))
    for lg in range(n_slab):
        ubuf[lg, 32:32 + tq, :] = u[:, lg * LANES:(lg + 1) * LANES]
    cst_ref[...] = u[tq - HALO:, :]

    rows_per_unit = CONV_STRIDE * SUBLANES
    for lg in range(n_slab):
        lanes = slice(lg * LANES, (lg + 1) * LANES)
        for base in range(0, tq, rows_per_unit):
            w = {}
            acc = [jnp.broadcast_to(cb_ref[:, lanes], (SUBLANES, LANES))] * CONV_STRIDE
            for off in range(CONV_STRIDE - 1 + CONV_W):
                win = ubuf[lg, pl.ds(base + off + 2, SUBLANES, stride=CONV_STRIDE), :]
                if off < CONV_W:
                    w[off] = jnp.broadcast_to(cw_ref[off:off + 1, lanes], (SUBLANES, LANES))
                for r in range(CONV_STRIDE):
                    if 0 <= off - r < CONV_W:
                        acc[r] = acc[r] + w[off - r] * win
            for r in range(CONV_STRIDE):
                cbuf[lg, pl.ds(base + r, SUBLANES, stride=CONV_STRIDE), :] = acc[r]

    cos, slo, shi = cos_ref[...], slo_ref[...], shi_ref[...]
    chunk = 4 * LANES

    def project(col):
        return _dot(h, win_ref[:, col:col + chunk])

    def kv_chunk(z):
        rk = jnp.concatenate(_rope(z[:, 0:KV_W], cos, slo, shi), axis=-1)
        zv = z[:, KV_W:2 * KV_W]
        for kv in range(N_KV):
            kdup[kv, WINDOW:WINDOW + tq, :] = _dup_halves(rk, kv).astype(BF16)
            vdup[kv, WINDOW:WINDOW + tq, :] = _dup_halves(zv, kv).astype(BF16)
        kw_ref[...] = rk[tq - WINDOW:, :]
        vw_ref[...] = zv[tq - WINDOW:, :]

    def q_chunk(c, z):
        for g, rq in enumerate(_rope(z, cos, slo, shi)):
            col = c * chunk + g * LANES
            qbuf[:, col:col + LANES] = (rq * (HEAD_DIM ** -0.5)).astype(BF16)

    def gate_chunk(c, z):
        gbuf[:, c * chunk:(c + 1) * chunk] = jax.nn.sigmoid(z)

    kv_chunk(project(O_K))
    for c in range(D // chunk):
        q_chunk(c, project(O_Q + c * chunk))
    for c in range(2 * D // chunk):
        gate_chunk(c, project(O_GC + c * chunk))

    c = jnp.concatenate([cbuf[lg] for lg in range(n_slab)], axis=-1)
    cs = _ln_silu(c, lng_ref[...], lnb_ref[...]).astype(BF16)
    conv_out = _dot(cs, wco_ref[...])

    row = lax.broadcasted_iota(jnp.int32, (WINDOW, 2 * WINDOW), 0)
    col = lax.broadcasted_iota(jnp.int32, (WINDOW, 2 * WINDOW), 1)
    band = (col > row) & (col <= row + WINDOW)
    low_q = lax.broadcasted_iota(jnp.int32, (WINDOW, LANES), 1) < HEAD_DIM
    low_kv = lax.broadcasted_iota(jnp.int32, (2 * WINDOW, LANES), 1) < HEAD_DIM
    zero_q = jnp.zeros((WINDOW, LANES), BF16)
    zero_kv = jnp.zeros((2 * WINDOW, LANES), BF16)
    sink_slot = lax.broadcasted_iota(jnp.int32, (WINDOW, LANES), 1) == 0
    not_key0 = lax.broadcasted_iota(jnp.int32, (2 * WINDOW, LANES), 0) > 0
    no_key = jnp.full((WINDOW, LANES), NEG_INF, F32)
    ones_bd = jnp.concatenate([low_kv, ~low_kv], axis=0).astype(F32).astype(BF16)
    for i in range(tq // WINDOW):
        r0 = i * WINDOW
        mask = band if i > 0 else band & (col >= jnp.where(t > 0, 0, WINDOW))
        for kv in range(N_KV):
            kd = kdup[kv, r0:r0 + 2 * WINDOW, :]
            vd = vdup[kv, r0:r0 + 2 * WINDOW, :]
            v_bd = jnp.concatenate([jnp.where(low_kv & not_key0, vd, zero_kv),
                                    jnp.where(~low_kv & not_key0, vd, zero_kv)], axis=0)
            v_aug = jnp.concatenate([v_bd, ones_bd], axis=-1)
            q_rows = []
            for pair in range(GROUP // 2):
                tile_col = (kv * (GROUP // 2) + pair) * LANES
                qp = qbuf[r0:r0 + WINDOW, tile_col:tile_col + LANES]
                q_rows += [jnp.where(low_q, qp, zero_q), jnp.where(low_q, zero_q, qp)]
            s_all = lax.dot_general(jnp.concatenate(q_rows, axis=0), kd, (((1,), (1,)), ((), ())),
                                    preferred_element_type=F32)
            probs = []
            for g in range(GROUP):
                fill = jnp.concatenate([jnp.where(sink_slot, sinks_ref[kv * GROUP + g], no_key), no_key], axis=-1)
                s = jnp.where(mask, s_all[g * WINDOW:(g + 1) * WINDOW], fill)
                probs.append(jnp.exp(s - jnp.max(s, axis=-1, keepdims=True)).astype(BF16))
            for pair in range(GROUP // 2):
                tile_col = (kv * (GROUP // 2) + pair) * LANES
                oa = _dot(jnp.concatenate(probs[2 * pair:2 * pair + 2], axis=-1), v_aug)
                obuf[r0:r0 + WINDOW, tile_col:tile_col + LANES] = (oa[:, 0:LANES] / oa[:, LANES:]).astype(BF16)
    attn_out = _dot(obuf[...], wao_ref[...])

    merged = gbuf[:, 0:D] * conv_out + gbuf[:, D:2 * D] * attn_out
    y = _dot(merged.astype(BF16), wo_ref[...])
    y_ref[...] = x + _rmsnorm(y, post_ref[...])


def _mix_prompt(x, p, tables):
    b, t, _ = x.shape
    tq = MIX_TQ
    assert t % tq == 0 and tq % WINDOW == 0 and tq % (CONV_STRIDE * SUBLANES) == 0
    tile = lambda bi, ti: (bi, ti, 0)
    per_seq = lambda bi, ti: (bi, 0, 0)
    tab = pl.BlockSpec((tq, LANES), lambda bi, ti: (ti, 0))
    return pl.pallas_call(
        _mix_prompt_kernel,
        grid=(b, t // tq),
        in_specs=[
            pl.BlockSpec((None, tq, D), tile),
            _resident((1, D)),
            _resident((D, IN_COLS)),
            _resident((CONV_W, D)),
            _resident((1, D)),
            _resident((1, D)),
            _resident((1, D)),
            _resident((D, D)),
            pl.BlockSpec(memory_space=pltpu.SMEM),
            _resident((D, D)),
            _resident((D, D)),
            _resident((1, D)),
            tab, tab, tab,
        ],
        out_specs=[
            pl.BlockSpec((None, tq, D), tile),
            pl.BlockSpec((None, HALO, D), per_seq),
            pl.BlockSpec((None, WINDOW, KV_W), per_seq),
            pl.BlockSpec((None, WINDOW, KV_W), per_seq),
        ],
        out_shape=[
            jax.ShapeDtypeStruct((b, t, D), F32),
            jax.ShapeDtypeStruct((b, HALO, D), F32),
            jax.ShapeDtypeStruct((b, WINDOW, KV_W), F32),
            jax.ShapeDtypeStruct((b, WINDOW, KV_W), F32),
        ],
        scratch_shapes=[
            pltpu.VMEM((D // LANES, 32 + tq, LANES), F32),
            pltpu.VMEM((D // LANES, tq, LANES), F32),
            pltpu.VMEM((tq, D), BF16),
            pltpu.VMEM((N_KV, WINDOW + tq, LANES), BF16),
            pltpu.VMEM((N_KV, WINDOW + tq, LANES), BF16),
            pltpu.VMEM((tq, 2 * D), F32),
            pltpu.VMEM((tq, D), BF16),
        ],
        compiler_params=pltpu.CompilerParams(
            dimension_semantics=("arbitrary", "arbitrary"), vmem_limit_bytes=VMEM_LIMIT),
        name="mix_prompt",
    )(x, p["mix_pre_g"], p["w_in"], p["conv_dw_w"], p["conv_dw_b"], p["conv_ln_g"], p["conv_ln_b"],
      p["w_conv_out"], p["attn_sinks"], p["w_attn_out"], p["w_out"], p["mix_post_g"], *tables)


def _mix_sample_kernel(x_ref, st_ref, kt_ref, vt_ref, pre_ref, win_ref, cw_ref, cb_ref, lng_ref, lnb_ref,
                       wco_ref, sinkcol_ref, wao_ref, wo_ref, post_ref, cos_ref, slo_ref, shi_ref,
                       y_ref, cst_ref, kto_ref, vto_ref,
                       ubuf, cbuf, qf, qm, sbuf, pbuf, obuf, of):
    nt = x_ref.shape[0]
    s_new = SUBLANES
    sb = nt // s_new
    l_buf = kt_ref.shape[2]
    n_slab = D // LANES

    x = x_ref[...]
    h = _rmsnorm(x, pre_ref[...]).astype(BF16)

    u = _dot(h, win_ref[:, 0:D]) * jax.nn.sigmoid(_dot(h, win_ref[:, O_GLU_G:O_GLU_G + D]))
    for lg in range(n_slab):
        ubuf[lg] = u[:, lg * LANES:(lg + 1) * LANES]

    def conv_row(i, lg):
        if i < HALO:
            return st_ref[i, :, lg * LANES:(lg + 1) * LANES]
        return ubuf[lg, pl.ds(i - HALO, sb, stride=s_new), :]

    for lg in range(n_slab):
        lanes = slice(lg * LANES, (lg + 1) * LANES)
        w = [jnp.broadcast_to(cw_ref[j:j + 1, lanes], (sb, LANES)) for j in range(CONV_W)]
        acc = [jnp.broadcast_to(cb_ref[:, lanes], (sb, LANES))] * s_new
        for i in range(HALO + s_new):
            row_i = conv_row(i, lg)
            for tok in range(s_new):
                if 0 <= i - tok < CONV_W:
                    acc[tok] = acc[tok] + w[i - tok] * row_i
            if i >= s_new:
                cst_ref[i - s_new, :, lanes] = row_i
        for tok in range(s_new):
            cbuf[lg, pl.ds(tok, sb, stride=s_new), :] = acc[tok]
    c = jnp.concatenate([cbuf[lg] for lg in range(n_slab)], axis=-1)
    cs = _ln_silu(c, lng_ref[...], lnb_ref[...]).astype(BF16)
    conv_out = _dot(cs, wco_ref[...])

    cos, slo, shi = cos_ref[...], slo_ref[...], shi_ref[...]
    zq = _dot(h, win_ref[:, O_Q:O_Q + D])
    for g, rq in enumerate(_rope(zq, cos, slo, shi)):
        qf[:, g * LANES:(g + 1) * LANES] = rq * (HEAD_DIM ** -0.5)
    rk = jnp.concatenate(_rope(_dot(h, win_ref[:, O_K:O_K + KV_W]), cos, slo, shi), axis=-1)
    zv = _dot(h, win_ref[:, O_V:O_V + KV_W])
    k_new, v_new = rk.astype(BF16), zv.astype(BF16)
    k_new_t, v_new_t = rk.T, zv.T

    n_rows = N_HEADS * s_new
    nt_dims = (((1,), (1,)), ((), ()))
    low_half = lax.broadcasted_iota(jnp.int32, (nt, LANES), 1) < HEAD_DIM
    zeros = jnp.zeros((sb, s_new, LANES), F32)

    for head in range(N_HEADS):
        kv = head // GROUP
        src = qf[:, (head // 2) * LANES:(head // 2 + 1) * LANES]
        if head % 2 != kv % 2:
            src = pltpu.roll(src, HEAD_DIM, 1)
        piece = jnp.where(low_half if kv % 2 == 0 else ~low_half, src, 0.0).reshape(sb, s_new, LANES)
        qm[:, head * s_new:(head + 1) * s_new, :] = jnp.concatenate(
            [piece, zeros] if kv // 2 == 0 else [zeros, piece], axis=-1)

    keep_old = lax.broadcasted_iota(jnp.int32, (KV_W, LANES), 1) < l_buf - s_new
    shift = l_buf - s_new

    def new_columns(z_t, b):
        amount = (shift - b * s_new) % LANES
        return pltpu.roll(z_t, amount, 1) if amount else z_t

    for b in range(sb):
        qb = qm[b].astype(BF16)
        kt = kt_ref[b]
        sbuf[b] = jnp.concatenate([_dot(qb, kt.astype(BF16)),
                                   lax.dot_general(qb, k_new, nt_dims, preferred_element_type=F32)], axis=-1)
        kto_ref[b] = jnp.where(keep_old, pltpu.roll(kt, shift, 1), new_columns(k_new_t, b))

    shape3 = (sb, n_rows, l_buf + nt)
    seq = lax.broadcasted_iota(jnp.int32, shape3, 0)
    tok = jnp.bitwise_and(lax.broadcasted_iota(jnp.int32, shape3, 1), s_new - 1)
    col = lax.broadcasted_iota(jnp.int32, shape3, 2)
    first_new = l_buf + seq * s_new
    mask = ((col < l_buf) & (col > tok + l_buf - WINDOW)) | ((col >= first_new) & (col <= first_new + tok))
    s = jnp.where(mask, sbuf[...], NEG_INF)
    sink = sinkcol_ref[...].reshape(1, n_rows, 1)
    m = jnp.maximum(jnp.max(s, axis=-1, keepdims=True), sink)
    p = jnp.exp(s - m)
    den = jnp.sum(p, axis=-1, keepdims=True) + jnp.exp(sink - m)
    pbuf[...] = (p * (1.0 / den)).astype(BF16)

    for b in range(sb):
        vt = vt_ref[b]
        pb = pbuf[b]
        obuf[b] = (lax.dot_general(pb[:, 0:l_buf], vt.astype(BF16), nt_dims, preferred_element_type=F32)
                   + _dot(pb[:, l_buf:], v_new))
        vto_ref[b] = jnp.where(keep_old, pltpu.roll(vt, shift, 1), new_columns(v_new_t, b))

    for pair in range(N_HEADS // 2):
        kv = (2 * pair) // GROUP
        halves = []
        for head in (2 * pair, 2 * pair + 1):
            blk = obuf[:, head * s_new:(head + 1) * s_new, (kv // 2) * LANES:(kv // 2 + 1) * LANES]
            blk = blk.reshape(nt, LANES)
            halves.append(pltpu.roll(blk, HEAD_DIM, 1) if head % 2 != kv % 2 else blk)
        of[:, pair * LANES:(pair + 1) * LANES] = jnp.where(low_half, halves[0], halves[1])
    attn_out = _dot(of[...].astype(BF16), wao_ref[...])

    g_conv = jax.nn.sigmoid(_dot(h, win_ref[:, O_GC:O_GC + D]))
    g_attn = jax.nn.sigmoid(_dot(h, win_ref[:, O_GA:O_GA + D]))
    y = _dot((g_conv * conv_out + g_attn * attn_out).astype(BF16), wo_ref[...])
    y_ref[...] = x + _rmsnorm(y, post_ref[...])


def _mix_sample(x2d, state_t, kt, vt, p, tables):
    nb, _, l_buf = kt.shape
    s_new = x2d.shape[0] // nb
    sb = SAMPLE_SB
    nt = sb * s_new
    assert s_new == SUBLANES and l_buf == LANES and nt == LANES and nb % sb == 0
    sink_col = jnp.repeat(p["attn_sinks"], s_new).reshape(N_HEADS * s_new, 1)
    rows = lambda i: (i, 0)
    seqs = lambda i: (i, 0, 0)
    taps = lambda i: (0, i, 0)
    tab = pl.BlockSpec((nt, LANES), lambda i: (0, 0), pipeline_mode=pl.Buffered(1))
    return pl.pallas_call(
        _mix_sample_kernel,
        grid=(nb // sb,),
        in_specs=[
            pl.BlockSpec((nt, D), rows),
            pl.BlockSpec((HALO, sb, D), taps),
            pl.BlockSpec((sb, KV_W, l_buf), seqs),
            pl.BlockSpec((sb, KV_W, l_buf), seqs),
            _resident((1, D)),
            _resident((D, IN_COLS)),
            _resident((CONV_W, D)),
            _resident((1, D)),
            _resident((1, D)),
            _resident((1, D)),
            _resident((D, D)),
            _resident((N_HEADS * s_new, 1)),
            _resident((D, D)),
            _resident((D, D)),
            _resident((1, D)),
            tab, tab, tab,
        ],
        out_specs=[
            pl.BlockSpec((nt, D), rows),
            pl.BlockSpec((HALO, sb, D), taps),
            pl.BlockSpec((sb, KV_W, l_buf), seqs),
            pl.BlockSpec((sb, KV_W, l_buf), seqs),
        ],
        out_shape=[
            jax.ShapeDtypeStruct(x2d.shape, F32),
            jax.ShapeDtypeStruct((HALO, nb, D), F32),
            jax.ShapeDtypeStruct((nb, KV_W, l_buf), F32),
            jax.ShapeDtypeStruct((nb, KV_W, l_buf), F32),
        ],
        scratch_shapes=[
            pltpu.VMEM((D // LANES, nt, LANES), F32),
            pltpu.VMEM((D // LANES, nt, LANES), F32),
            pltpu.VMEM((nt, D), F32),
            pltpu.VMEM((sb, N_HEADS * s_new, KV_W), F32),
            pltpu.VMEM((sb, N_HEADS * s_new, l_buf + nt), F32),
            pltpu.VMEM((sb, N_HEADS * s_new, l_buf + nt), BF16),
            pltpu.VMEM((sb, N_HEADS * s_new, KV_W), F32),
            pltpu.VMEM((nt, D), F32),
        ],
        compiler_params=pltpu.CompilerParams(
            dimension_semantics=("arbitrary",), vmem_limit_bytes=VMEM_LIMIT),
        name="mix_sample",
    )(x2d, state_t, kt, vt, p["mix_pre_g"], p["w_in"], p["conv_dw_w"], p["conv_dw_b"], p["conv_ln_g"],
      p["conv_ln_b"], p["w_conv_out"], sink_col, p["w_attn_out"], p["w_out"], p["mix_post_g"], *tables)


def kernel(x_prompt, x_sample, state_conv, cache_k_win, cache_v_win, ffn1_pre_g, ffn1_w_up, ffn1_w_down, ffn1_post_g, mix_pre_g, w_in, conv_dw_w, conv_dw_b, conv_ln_g, conv_ln_b, w_conv_out, attn_sinks, w_attn_out, w_out, mix_post_g, ffn2_pre_g, ffn2_w_up, ffn2_w_down, ffn2_post_g):
    depth = w_in.shape[0]
    b, t, _ = x_prompt.shape
    nb, s_new, _ = x_sample.shape
    l_buf = cache_k_win.shape[2]
    later_weights = dict(w_in=w_in, w_conv_out=w_conv_out, w_attn_out=w_attn_out, w_out=w_out,
                         ffn2_w_up=ffn2_w_up, ffn2_w_down=ffn2_w_down)
    other = dict(ffn1_pre_g=ffn1_pre_g, ffn1_post_g=ffn1_post_g, mix_pre_g=mix_pre_g, conv_dw_w=conv_dw_w,
                 conv_dw_b=conv_dw_b, conv_ln_g=conv_ln_g, conv_ln_b=conv_ln_b, attn_sinks=attn_sinks,
                 mix_post_g=mix_post_g, ffn2_pre_g=ffn2_pre_g, ffn2_post_g=ffn2_post_g)
    tab_p = _rope_tables(jnp.arange(t))
    tab_s = _rope_tables(jnp.tile(PAST_LEN + jnp.arange(s_new), SAMPLE_SB))

    hp = x_prompt.reshape(b * t, D)
    hs = x_sample.reshape(nb * s_new, D)
    outs = [[] for _ in range(6)]
    for l in range(depth):
        p = {"ffn1_w_up": ffn1_w_up[l].astype(BF16), "ffn1_w_down": ffn1_w_down[l].astype(BF16)}
        for n, w in other.items():
            p[n] = w[l] if w[l].ndim == 2 or n == "attn_sinks" else w[l].reshape(1, -1)

        hp, converted = _ffn(hp, p["ffn1_pre_g"], p["ffn1_w_up"], p["ffn1_w_down"], p["ffn1_post_g"],
                             FFN_TM_FIRST, tuple(w[l] for w in later_weights.values()))
        p.update(zip(later_weights, converted))
        hp, c1, k1, v1 = _mix_prompt(hp.reshape(b, t, D), p, tab_p)
        hp, _ = _ffn(hp.reshape(b * t, D), p["ffn2_pre_g"], p["ffn2_w_up"], p["ffn2_w_down"], p["ffn2_post_g"], FFN_TM)

        rows_last = lambda w: jnp.transpose(w, (0, 2, 3, 1)).reshape(nb, KV_W, l_buf)
        rows_back = lambda w: jnp.transpose(w.reshape(nb, N_KV, HEAD_DIM, l_buf), (0, 3, 1, 2))
        hs, _ = _ffn(hs, p["ffn1_pre_g"], p["ffn1_w_up"], p["ffn1_w_down"], p["ffn1_post_g"], FFN_TM_FIRST)
        hs, c2, k2, v2 = _mix_sample(hs, jnp.transpose(state_conv[l], (1, 0, 2)),
                                     rows_last(cache_k_win[l]), rows_last(cache_v_win[l]), p, tab_s)
        hs, _ = _ffn(hs, p["ffn2_pre_g"], p["ffn2_w_up"], p["ffn2_w_down"], p["ffn2_post_g"], FFN_TM_FIRST)

        kv_shape = lambda n: (n, -1, N_KV, HEAD_DIM)
        for acc, val in zip(outs, (c1, k1.reshape(kv_shape(b)), v1.reshape(kv_shape(b)),
                                   jnp.transpose(c2, (1, 0, 2)), rows_back(k2), rows_back(v2))):
            acc.append(val)
    return (hp.reshape(b, t, D), hs.reshape(nb, s_new, D), *(jnp.stack(o) for o in outs))
```

```python
import math

import jax
import jax.numpy as jnp
from jax import lax
from jax.experimental import pallas as pl
from jax.experimental.pallas import tpu as pltpu

D = 1024
N_HEADS = 16
HEAD_DIM = 64
N_KV = 4
GROUP = 4
ROT_DIM = 16
ROPE_THETA = 500000.0
WINDOW = 128
CONV_W = 31
HALO = CONV_W - 1
D_FF = 2816
KV_W = N_KV * HEAD_DIM
EPS = 1e-6
NEG_INF = -1e30
PAST_LEN = 8192
O_GLU_G = D
O_Q = 2 * D
O_K = O_Q + D
O_V = O_K + KV_W
O_GC = O_V + KV_W
O_GA = O_GC + D
IN_COLS = O_GA + D

LANES = 128
SUBLANES = 8
VMEM_LIMIT = 56 * 1024 * 1024

FFN_TM = 1024
FFN_TM_FIRST = 512
MIX_TQ = 512
CONV_STRIDE = 4
SAMPLE_SB = 16

F32 = jnp.float32
BF16 = jnp.bfloat16


def _dot(a, b):
    return jnp.dot(a, b, preferred_element_type=F32)


def _rmsnorm(x, g):
    r = lax.rsqrt(jnp.mean(x * x, axis=-1, keepdims=True) + EPS)
    return x * r * g


def _resident(shape):
    return pl.BlockSpec(shape, lambda *_: (0,) * len(shape), pipeline_mode=pl.Buffered(1))


def _ffn_kernel(x_ref, pre_ref, wup_ref, wdn_ref, post_ref, *rest):
    n_cast = len(rest) // 2
    cast_src, o_ref, cast_dst = rest[:n_cast], rest[n_cast], rest[n_cast + 1:]
    x = x_ref[...]
    h = _rmsnorm(x, pre_ref[...]).astype(BF16)
    gu = _dot(h, wup_ref[...])
    a = (jax.nn.silu(gu[:, 0:D_FF]) * gu[:, D_FF:]).astype(BF16)
    y = _dot(a, wdn_ref[...])
    o_ref[...] = x + 0.5 * _rmsnorm(y, post_ref[...])
    for src, dst in zip(cast_src, cast_dst):
        dst[...] = src[...].astype(BF16)


def _ffn(x2d, pre_g, w_up, w_down, post_g, tm, to_bf16=()):
    n = x2d.shape[0]
    steps = n // tm
    assert n % tm == 0 and D_FF % LANES == 0
    bf16_rows = 2 * SUBLANES
    tile = pl.BlockSpec((tm, D), lambda i: (i, 0))

    def slab(w):
        blocks = steps if w.shape[0] % (steps * bf16_rows) == 0 else steps // 2
        assert w.shape[0] % (blocks * bf16_rows) == 0
        return pl.BlockSpec((w.shape[0] // blocks, w.shape[1]), lambda i: (jnp.minimum(i, blocks - 1), 0))

    slabs = [slab(w) for w in to_bf16]
    y, *converted = pl.pallas_call(
        _ffn_kernel,
        grid=(steps,),
        in_specs=[
            tile,
            _resident((1, D)),
            _resident((D, 2 * D_FF)),
            _resident((D_FF, D)),
            _resident((1, D)),
            *slabs,
        ],
        out_specs=[tile, *slabs],
        out_shape=[jax.ShapeDtypeStruct((n, D), F32)] + [jax.ShapeDtypeStruct(w.shape, BF16) for w in to_bf16],
        compiler_params=pltpu.CompilerParams(
            dimension_semantics=("arbitrary",), vmem_limit_bytes=VMEM_LIMIT),
        name="ffn",
    )(x2d, pre_g, w_up, w_down, post_g, *to_bf16)
    return y, converted


def _rope(z, cos, sin_lo, sin_hi):
    half = ROT_DIM // 2
    outs = []
    for g in range(z.shape[-1] // LANES):
        zg = z[:, g * LANES:(g + 1) * LANES]
        outs.append(zg * cos
                    + pltpu.roll(zg, LANES - half, 1) * sin_lo
                    + pltpu.roll(zg, half, 1) * sin_hi)
    return outs


def _rope_tables(pos):
    half = ROT_DIM // 2
    inv = jnp.exp(-math.log(ROPE_THETA) * jnp.arange(0, ROT_DIM, 2, dtype=F32) / ROT_DIM)
    ang = pos.astype(F32)[:, None] * inv[None, :]
    cos, sin = jnp.cos(ang), jnp.sin(ang)
    n = pos.shape[0]
    pad = jnp.zeros((n, HEAD_DIM - ROT_DIM), F32)
    zero = jnp.zeros((n, half), F32)
    cos_h = jnp.concatenate([cos, cos, pad + 1.0], axis=1)
    lo_h = jnp.concatenate([-sin, zero, pad], axis=1)
    hi_h = jnp.concatenate([zero, sin, pad], axis=1)
    rep = LANES // HEAD_DIM
    return tuple(jnp.tile(t, (1, rep)) for t in (cos_h, lo_h, hi_h))


def _ln_silu(c, g, b):
    mu = jnp.mean(c, axis=-1, keepdims=True)
    var = jnp.mean(jnp.square(c - mu), axis=-1, keepdims=True)
    return jax.nn.silu((c - mu) * lax.rsqrt(var + EPS) * g + b)


def _dup_halves(z, kv):
    src = z[:, (kv // 2) * LANES:(kv // 2 + 1) * LANES]
    other = pltpu.roll(src, HEAD_DIM, 1)
    low = lax.broadcasted_iota(jnp.int32, src.shape, 1) < HEAD_DIM
    return jnp.where(low, src, other) if kv % 2 == 0 else jnp.where(low, other, src)


def _mix_prompt_kernel(x_ref, pre_ref, win_ref, cw_ref, cb_ref, lng_ref, lnb_ref, wco_ref, sinks_ref,
                       wao_ref, wo_ref, post_ref, cos_ref, slo_ref, shi_ref,
                       y_ref, cst_ref, kw_ref, vw_ref,
                       ubuf, cbuf, qbuf, kdup, vdup, gbuf, obuf):
    tq = x_ref.shape[0]
    t = pl.program_id(1)
    n_slab = D // LANES

    @pl.when(t == 0)
    def _():
        ubuf[:, 0:32, :] = jnp.zeros((n_slab, 32, LANES), F32)
        kdup[:, 0:WINDOW, :] = jnp.zeros((N_KV, WINDOW, LANES), BF16)
        vdup[:, 0:WINDOW, :] = jnp.zeros((N_KV, WINDOW, LANES), BF16)

    @pl.when(t > 0)
    def _():
        ubuf[:, 0:32, :] = ubuf[:, tq:tq + 32, :]
        kdup[:, 0:WINDOW, :] = kdup[:, tq:tq + WINDOW, :]
        vdup[:, 0:WINDOW, :] = vdup[:, tq:tq + WINDOW, :]

    x = x_ref[...]
    h = _rmsnorm(x, pre_ref[...]).astype(BF16)

    u = _dot(h, win_ref[:, 0:D]) * jax.nn.sigmoid(_dot(h, win_ref[:, O_GLU_G:O_GLU_G + D]))
    for lg in range(n_slab):
        ubuf[lg, 32:32 + tq, :] = u[:, lg * LANES:(lg + 1) * LANES]
    cst_ref[...] = u[tq - HALO:, :]

    rows_per_unit = CONV_STRIDE * SUBLANES
    for lg in range(n_slab):
        lanes = slice(lg * LANES, (lg + 1) * LANES)
        for base in range(0, tq, rows_per_unit):
            w = {}
            acc = [jnp.broadcast_to(cb_ref[:, lanes], (SUBLANES, LANES))] * CONV_STRIDE
            for off in range(CONV_STRIDE - 1 + CONV_W):
                win = ubuf[lg, pl.ds(base + off + 2, SUBLANES, stride=CONV_STRIDE), :]
                if off < CONV_W:
                    w[off] = jnp.broadcast_to(cw_ref[off:off + 1, lanes], (SUBLANES, LANES))
                for r in range(CONV_STRIDE):
                    if 0 <= off - r < CONV_W:
                        acc[r] = acc[r] + w[off - r] * win
            for r in range(CONV_STRIDE):
                cbuf[lg, pl.ds(base + r, SUBLANES, stride=CONV_STRIDE), :] = acc[r]

    cos, slo, shi = cos_ref[...], slo_ref[...], shi_ref[...]
    chunk = 4 * LANES

    def project(col):
        return _dot(h, win_ref[:, col:col + chunk])

    def kv_chunk(z):
        rk = jnp.concatenate(_rope(z[:, 0:KV_W], cos, slo, shi), axis=-1)
        zv = z[:, KV_W:2 * KV_W]
        for kv in range(N_KV):
            kdup[kv, WINDOW:WINDOW + tq, :] = _dup_halves(rk, kv).astype(BF16)
            vdup[kv, WINDOW:WINDOW + tq, :] = _dup_halves(zv, kv).astype(BF16)
        kw_ref[...] = rk[tq - WINDOW:, :]
        vw_ref[...] = zv[tq - WINDOW:, :]

    def q_chunk(c, z):
        for g, rq in enumerate(_rope(z, cos, slo, shi)):
            col = c * chunk + g * LANES
            qbuf[:, col:col + LANES] = (rq * (HEAD_DIM ** -0.5)).astype(BF16)

    def gate_chunk(c, z):
        gbuf[:, c * chunk:(c + 1) * chunk] = jax.nn.sigmoid(z)

    kv_chunk(project(O_K))
    for c in range(D // chunk):
        q_chunk(c, project(O_Q + c * chunk))
    for c in range(2 * D // chunk):
        gate_chunk(c, project(O_GC + c * chunk))

    c = jnp.concatenate([cbuf[lg] for lg in range(n_slab)], axis=-1)
    cs = _ln_silu(c, lng_ref[...], lnb_ref[...]).astype(BF16)
    conv_out = _dot(cs, wco_ref[...])

    row = lax.broadcasted_iota(jnp.int32, (WINDOW, 2 * WINDOW), 0)
    col = lax.broadcasted_iota(jnp.int32, (WINDOW, 2 * WINDOW), 1)
    band = (col > row) & (col <= row + WINDOW)
    low_q = lax.broadcasted_iota(jnp.int32, (WINDOW, LANES), 1) < HEAD_DIM
    low_kv = lax.broadcasted_iota(jnp.int32, (2 * WINDOW, LANES), 1) < HEAD_DIM
    zero_q = jnp.zeros((WINDOW, LANES), BF16)
    zero_kv = jnp.zeros((2 * WINDOW, LANES), BF16)
    sink_slot = lax.broadcasted_iota(jnp.int32, (WINDOW, LANES), 1) == 0
    not_key0 = lax.broadcasted_iota(jnp.int32, (2 * WINDOW, LANES), 0) > 0
    no_key = jnp.full((WINDOW, LANES), NEG_INF, F32)
    ones_bd = jnp.concatenate([low_kv, ~low_kv], axis=0).astype(F32).astype(BF16)
    for i in range(tq // WINDOW):
        r0 = i * WINDOW
        mask = band if i > 0 else band & (col >= jnp.where(t > 0, 0, WINDOW))
        for kv in range(N_KV):
            kd = kdup[kv, r0:r0 + 2 * WINDOW, :]
            vd = vdup[kv, r0:r0 + 2 * WINDOW, :]
            v_bd = jnp.concatenate([jnp.where(low_kv & not_key0, vd, zero_kv),
                                    jnp.where(~low_kv & not_key0, vd, zero_kv)], axis=0)
            v_aug = jnp.concatenate([v_bd, ones_bd], axis=-1)
            q_rows = []
            for pair in range(GROUP // 2):
                tile_col = (kv * (GROUP // 2) + pair) * LANES
                qp = qbuf[r0:r0 + WINDOW, tile_col:tile_col + LANES]
                q_rows += [jnp.where(low_q, qp, zero_q), jnp.where(low_q, zero_q, qp)]
            s_all = lax.dot_general(jnp.concatenate(q_rows, axis=0), kd, (((1,), (1,)), ((), ())),
                                    preferred_element_type=F32)
            probs = []
            for g in range(GROUP):
                fill = jnp.concatenate([jnp.where(sink_slot, sinks_ref[kv * GROUP + g], no_key), no_key], axis=-1)
                s = jnp.where(mask, s_all[g * WINDOW:(g + 1) * WINDOW], fill)
                probs.append(jnp.exp(s - jnp.max(s, axis=-1, keepdims=True)).astype(BF16))
            for pair in range(GROUP // 2):
                tile_col = (kv * (GROUP // 2) + pair) * LANES
                oa = _dot(jnp.concatenate(probs[2 * pair:2 * pair + 2], axis=-1), v_aug)
                obuf[r0:r0 + WINDOW, tile_col:tile_col + LANES] = (oa[:, 0:LANES] / oa[:, LANES:]).astype(BF16)
    attn_out = _dot(obuf[...], wao_ref[...])

    merged = gbuf[:, 0:D] * conv_out + gbuf[:, D:2 * D] * attn_out
    y = _dot(merged.astype(BF16), wo_ref[...])
    y_ref[...] = x + _rmsnorm(y, post_ref[...])


def _mix_prompt(x, p, tables):
    b, t, _ = x.shape
    tq = MIX_TQ
    assert t % tq == 0 and tq % WINDOW == 0 and tq % (CONV_STRIDE * SUBLANES) == 0
    tile = lambda bi, ti: (bi, ti, 0)
    per_seq = lambda bi, ti: (bi, 0, 0)
    tab = pl.BlockSpec((tq, LANES), lambda bi, ti: (ti, 0))
    return pl.pallas_call(
        _mix_prompt_kernel,
        grid=(b, t // tq),
        in_specs=[
            pl.BlockSpec((None, tq, D), tile),
            _resident((1, D)),
            _resident((D, IN_COLS)),
            _resident((CONV_W, D)),
            _resident((1, D)),
            _resident((1, D)),
            _resident((1, D)),
            _resident((D, D)),
            pl.BlockSpec(memory_space=pltpu.SMEM),
            _resident((D, D)),
            _resident((D, D)),
            _resident((1, D)),
            tab, tab, tab,
        ],
        out_specs=[
            pl.BlockSpec((None, tq, D), tile),
            pl.BlockSpec((None, HALO, D), per_seq),
            pl.BlockSpec((None, WINDOW, KV_W), per_seq),
            pl.BlockSpec((None, WINDOW, KV_W), per_seq),
        ],
        out_shape=[
            jax.ShapeDtypeStruct((b, t, D), F32),
            jax.ShapeDtypeStruct((b, HALO, D), F32),
            jax.ShapeDtypeStruct((b, WINDOW, KV_W), F32),
            jax.ShapeDtypeStruct((b, WINDOW, KV_W), F32),
        ],
        scratch_shapes=[
            pltpu.VMEM((D // LANES, 32 + tq, LANES), F32),
            pltpu.VMEM((D // LANES, tq, LANES), F32),
            pltpu.VMEM((tq, D), BF16),
            pltpu.VMEM((N_KV, WINDOW + tq, LANES), BF16),
            pltpu.VMEM((N_KV, WINDOW + tq, LANES), BF16),
            pltpu.VMEM((tq, 2 * D), F32),
            pltpu.VMEM((tq, D), BF16),
        ],
        compiler_params=pltpu.CompilerParams(
            dimension_semantics=("arbitrary", "arbitrary"), vmem_limit_bytes=VMEM_LIMIT),
        name="mix_prompt",
    )(x, p["mix_pre_g"], p["w_in"], p["conv_dw_w"], p["conv_dw_b"], p["conv_ln_g"], p["conv_ln_b"],
      p["w_conv_out"], p["attn_sinks"], p["w_attn_out"], p["w_out"], p["mix_post_g"], *tables)


def _mix_sample_kernel(x_ref, st_ref, kt_ref, vt_ref, pre_ref, win_ref, cw_ref, cb_ref, lng_ref, lnb_ref,
                       wco_ref, sinkcol_ref, wao_ref, wo_ref, post_ref, cos_ref, slo_ref, shi_ref,
                       y_ref, cst_ref, kto_ref, vto_ref,
                       ubuf, cbuf, qf, qm, sbuf, pbuf, obuf, of):
    nt = x_ref.shape[0]
    s_new = SUBLANES
    sb = nt // s_new
    l_buf = kt_ref.shape[2]
    n_slab = D // LANES

    x = x_ref[...]
    h = _rmsnorm(x, pre_ref[...]).astype(BF16)

    u = _dot(h, win_ref[:, 0:D]) * jax.nn.sigmoid(_dot(h, win_ref[:, O_GLU_G:O_GLU_G + D]))
    for lg in range(n_slab):
        ubuf[lg] = u[:, lg * LANES:(lg + 1) * LANES]

    def conv_row(i, lg):
        if i < HALO:
            return st_ref[i, :, lg * LANES:(lg + 1) * LANES]
        return ubuf[lg, pl.ds(i - HALO, sb, stride=s_new), :]

    for lg in range(n_slab):
        lanes = slice(lg * LANES, (lg + 1) * LANES)
        w = [jnp.broadcast_to(cw_ref[j:j + 1, lanes], (sb, LANES)) for j in range(CONV_W)]
        acc = [jnp.broadcast_to(cb_ref[:, lanes], (sb, LANES))] * s_new
        for i in range(HALO + s_new):
            row_i = conv_row(i, lg)
            for tok in range(s_new):
                if 0 <= i - tok < CONV_W:
                    acc[tok] = acc[tok] + w[i - tok] * row_i
            if i >= s_new:
                cst_ref[i - s_new, :, lanes] = row_i
        for tok in range(s_new):
            cbuf[lg, pl.ds(tok, sb, stride=s_new), :] = acc[tok]
    c = jnp.concatenate([cbuf[lg] for lg in range(n_slab)], axis=-1)
    cs = _ln_silu(c, lng_ref[...], lnb_ref[...]).astype(BF16)
    conv_out = _dot(cs, wco_ref[...])

    cos, slo, shi = cos_ref[...], slo_ref[...], shi_ref[...]
    zq = _dot(h, win_ref[:, O_Q:O_Q + D])
    for g, rq in enumerate(_rope(zq, cos, slo, shi)):
        qf[:, g * LANES:(g + 1) * LANES] = rq * (HEAD_DIM ** -0.5)
    rk = jnp.concatenate(_rope(_dot(h, win_ref[:, O_K:O_K + KV_W]), cos, slo, shi), axis=-1)
    zv = _dot(h, win_ref[:, O_V:O_V + KV_W])
    k_new, v_new = rk.astype(BF16), zv.astype(BF16)
    k_new_t, v_new_t = rk.T, zv.T

    n_rows = N_HEADS * s_new
    nt_dims = (((1,), (1,)), ((), ()))
    low_half = lax.broadcasted_iota(jnp.int32, (nt, LANES), 1) < HEAD_DIM
    zeros = jnp.zeros((sb, s_new, LANES), F32)

    for head in range(N_HEADS):
        kv = head // GROUP
        src = qf[:, (head // 2) * LANES:(head // 2 + 1) * LANES]
        if head % 2 != kv % 2:
            src = pltpu.roll(src, HEAD_DIM, 1)
        piece = jnp.where(low_half if kv % 2 == 0 else ~low_half, src, 0.0).reshape(sb, s_new, LANES)
        qm[:, head * s_new:(head + 1) * s_new, :] = jnp.concatenate(
            [piece, zeros] if kv // 2 == 0 else [zeros, piece], axis=-1)

    keep_old = lax.broadcasted_iota(jnp.int32, (KV_W, LANES), 1) < l_buf - s_new
    shift = l_buf - s_new

    def new_columns(z_t, b):
        amount = (shift - b * s_new) % LANES
        return pltpu.roll(z_t, amount, 1) if amount else z_t

    for b in range(sb):
        qb = qm[b].astype(BF16)
        kt = kt_ref[b]
        sbuf[b] = jnp.concatenate([_dot(qb, kt.astype(BF16)),
                                   lax.dot_general(qb, k_new, nt_dims, preferred_element_type=F32)], axis=-1)
        kto_ref[b] = jnp.where(keep_old, pltpu.roll(kt, shift, 1), new_columns(k_new_t, b))

    shape3 = (sb, n_rows, l_buf + nt)
    seq = lax.broadcasted_iota(jnp.int32, shape3, 0)
    tok = jnp.bitwise_and(lax.broadcasted_iota(jnp.int32, shape3, 1), s_new - 1)
    col = lax.broadcasted_iota(jnp.int32, shape3, 2)
    first_new = l_buf + seq * s_new
    mask = ((col < l_buf) & (col > tok + l_buf - WINDOW)) | ((col >= first_new) & (col <= first_new + tok))
    s = jnp.where(mask, sbuf[...], NEG_INF)
    sink = sinkcol_ref[...].reshape(1, n_rows, 1)
    m = jnp.maximum(jnp.max(s, axis=-1, keepdims=True), sink)
    p = jnp.exp(s - m)
    den = jnp.sum(p, axis=-1, keepdims=True) + jnp.exp(sink - m)
    pbuf[...] = (p * (1.0 / den)).astype(BF16)

    for b in range(sb):
        vt = vt_ref[b]
        pb = pbuf[b]
        obuf[b] = (lax.dot_general(pb[:, 0:l_buf], vt.astype(BF16), nt_dims, preferred_element_type=F32)
                   + _dot(pb[:, l_buf:], v_new))
        vto_ref[b] = jnp.where(keep_old, pltpu.roll(vt, shift, 1), new_columns(v_new_t, b))

    for pair in range(N_HEADS // 2):
        kv = (2 * pair) // GROUP
        halves = []
        for head in (2 * pair, 2 * pair + 1):
            blk = obuf[:, head * s_new:(head + 1) * s_new, (kv // 2) * LANES:(kv // 2 + 1) * LANES]
            blk = blk.reshape(nt, LANES)
            halves.append(pltpu.roll(blk, HEAD_DIM, 1) if head % 2 != kv % 2 else blk)
        of[:, pair * LANES:(pair + 1) * LANES] = jnp.where(low_half, halves[0], halves[1])
    attn_out = _dot(of[...].astype(BF16), wao_ref[...])

    g_conv = jax.nn.sigmoid(_dot(h, win_ref[:, O_GC:O_GC + D]))
    g_attn = jax.nn.sigmoid(_dot(h, win_ref[:, O_GA:O_GA + D]))
    y = _dot((g_conv * conv_out + g_attn * attn_out).astype(BF16), wo_ref[...])
    y_ref[...] = x + _rmsnorm(y, post_ref[...])


def _mix_sample(x2d, state_t, kt, vt, p, tables):
    nb, _, l_buf = kt.shape
    s_new = x2d.shape[0] // nb
    sb = SAMPLE_SB
    nt = sb * s_new
    assert s_new == SUBLANES and l_buf == LANES and nt == LANES and nb % sb == 0
    sink_col = jnp.repeat(p["attn_sinks"], s_new).reshape(N_HEADS * s_new, 1)
    rows = lambda i: (i, 0)
    seqs = lambda i: (i, 0, 0)
    taps = lambda i: (0, i, 0)
    tab = pl.BlockSpec((nt, LANES), lambda i: (0, 0), pipeline_mode=pl.Buffered(1))
    return pl.pallas_call(
        _mix_sample_kernel,
        grid=(nb // sb,),
        in_specs=[
            pl.BlockSpec((nt, D), rows),
            pl.BlockSpec((HALO, sb, D), taps),
            pl.BlockSpec((sb, KV_W, l_buf), seqs),
            pl.BlockSpec((sb, KV_W, l_buf), seqs),
            _resident((1, D)),
            _resident((D, IN_COLS)),
            _resident((CONV_W, D)),
            _resident((1, D)),
            _resident((1, D)),
            _resident((1, D)),
            _resident((D, D)),
            _resident((N_HEADS * s_new, 1)),
            _resident((D, D)),
            _resident((D, D)),
            _resident((1, D)),
            tab, tab, tab,
        ],
        out_specs=[
            pl.BlockSpec((nt, D), rows),
            pl.BlockSpec((HALO, sb, D), taps),
            pl.BlockSpec((sb, KV_W, l_buf), seqs),
            pl.BlockSpec((sb, KV_W, l_buf), seqs),
        ],
        out_shape=[
            jax.ShapeDtypeStruct(x2d.shape, F32),
            jax.ShapeDtypeStruct((HALO, nb, D), F32),
            jax.ShapeDtypeStruct((nb, KV_W, l_buf), F32),
            jax.ShapeDtypeStruct((nb, KV_W, l_buf), F32),
        ],
        scratch_shapes=[
            pltpu.VMEM((D // LANES, nt, LANES), F32),
            pltpu.VMEM((D // LANES, nt, LANES), F32),
            pltpu.VMEM((nt, D), F32),
            pltpu.VMEM((sb, N_HEADS * s_new, KV_W), F32),
            pltpu.VMEM((sb, N_HEADS * s_new, l_buf + nt), F32),
            pltpu.VMEM((sb, N_HEADS * s_new, l_buf + nt), BF16),
            pltpu.VMEM((sb, N_HEADS * s_new, KV_W), F32),
            pltpu.VMEM((nt, D), F32),
        ],
        compiler_params=pltpu.CompilerParams(
            dimension_semantics=("arbitrary",), vmem_limit_bytes=VMEM_LIMIT),
        name="mix_sample",
    )(x2d, state_t, kt, vt, p["mix_pre_g"], p["w_in"], p["conv_dw_w"], p["conv_dw_b"], p["conv_ln_g"],
      p["conv_ln_b"], p["w_conv_out"], sink_col, p["w_attn_out"], p["w_out"], p["mix_post_g"], *tables)


def kernel(x_prompt, x_sample, state_conv, cache_k_win, cache_v_win, ffn1_pre_g, ffn1_w_up, ffn1_w_down, ffn1_post_g, mix_pre_g, w_in, conv_dw_w, conv_dw_b, conv_ln_g, conv_ln_b, w_conv_out, attn_sinks, w_attn_out, w_out, mix_post_g, ffn2_pre_g, ffn2_w_up, ffn2_w_down, ffn2_post_g):
    depth = w_in.shape[0]
    b, t, _ = x_prompt.shape
    nb, s_new, _ = x_sample.shape
    l_buf = cache_k_win.shape[2]
    later_weights = dict(w_in=w_in, w_conv_out=w_conv_out, w_attn_out=w_attn_out, w_out=w_out,
                         ffn2_w_up=ffn2_w_up, ffn2_w_down=ffn2_w_down)
    other = dict(ffn1_pre_g=ffn1_pre_g, ffn1_post_g=ffn1_post_g, mix_pre_g=mix_pre_g, conv_dw_w=conv_dw_w,
                 conv_dw_b=conv_dw_b, conv_ln_g=conv_ln_g, conv_ln_b=conv_ln_b, attn_sinks=attn_sinks,
                 mix_post_g=mix_post_g, ffn2_pre_g=ffn2_pre_g, ffn2_post_g=ffn2_post_g)
    tab_p = _rope_tables(jnp.arange(t))
    tab_s = _rope_tables(jnp.tile(PAST_LEN + jnp.arange(s_new), SAMPLE_SB))

    hp = x_prompt.reshape(b * t, D)
    hs = x_sample.reshape(nb * s_new, D)
    outs = [[] for _ in range(6)]
    for l in range(depth):
        p = {"ffn1_w_up": ffn1_w_up[l].astype(BF16), "ffn1_w_down": ffn1_w_down[l].astype(BF16)}
        for n, w in other.items():
            p[n] = w[l] if w[l].ndim == 2 or n == "attn_sinks" else w[l].reshape(1, -1)

        hp, converted = _ffn(hp, p["ffn1_pre_g"], p["ffn1_w_up"], p["ffn1_w_down"], p["ffn1_post_g"],
                             FFN_TM_FIRST, tuple(w[l] for w in later_weights.values()))
        p.update(zip(later_weights, converted))
        hp, c1, k1, v1 = _mix_prompt(hp.reshape(b, t, D), p, tab_p)
        hp, _ = _ffn(hp.reshape(b * t, D), p["ffn2_pre_g"], p["ffn2_w_up"], p["ffn2_w_down"], p["ffn2_post_g"], FFN_TM)

        rows_last = lambda w: jnp.transpose(w, (0, 2, 3, 1)).reshape(nb, KV_W, l_buf)
        rows_back = lambda w: jnp.transpose(w.reshape(nb, N_KV, HEAD_DIM, l_buf), (0, 3, 1, 2))
        hs, _ = _ffn(hs, p["ffn1_pre_g"], p["ffn1_w_up"], p["ffn1_w_down"], p["ffn1_post_g"], FFN_TM_FIRST)
        hs, c2, k2, v2 = _mix_sample(hs, jnp.transpose(state_conv[l], (1, 0, 2)),
                                     rows_last(cache_k_win[l]), rows_last(cache_v_win[l]), p, tab_s)
        hs, _ = _ffn(hs, p["ffn2_pre_g"], p["ffn2_w_up"], p["ffn2_w_down"], p["ffn2_post_g"], FFN_TM_FIRST)

        kv_shape = lambda n: (n, -1, N_KV, HEAD_DIM)
        for acc, val in zip(outs, (c1, k1.reshape(kv_shape(b)), v1.reshape(kv_shape(b)),
                                   jnp.transpose(c2, (1, 0, 2)), rows_back(k2), rows_back(v2))):
            acc.append(val)
    return (hp.reshape(b, t, D), hs.reshape(nb, s_new, D), *(jnp.stack(o) for o in outs))
```

```python
import functools
import math

import jax
import jax.numpy as jnp
from jax import lax
from jax.experimental import pallas as pl
from jax.experimental.pallas import tpu as pltpu

D = 1024
N_HEADS = 16
HEAD_DIM = 64
N_KV = 4
GROUP = 4
ROT_DIM = 16
ROPE_THETA = 500000.0
WINDOW = 128
CONV_W = 31
HALO = CONV_W - 1
D_FF = 2816
KV_W = N_KV * HEAD_DIM
EPS = 1e-6
NEG_INF = -1e30
PAST_LEN = 8192
O_GLU_G = D
O_Q = 2 * D
O_K = O_Q + D
O_V = O_K + KV_W
O_GC = O_V + KV_W
O_GA = O_GC + D
IN_COLS = O_GA + D

LANES = 128
SUBLANES = 8
VMEM_LIMIT = 56 * 1024 * 1024

FFN_TM = 1024
FFN_TM_FIRST = 512
MIX_TQ = 512
CONV_STRIDE = 4
SAMPLE_SB = 16

F32 = jnp.float32
BF16 = jnp.bfloat16


def _dot(a, b):
    return jnp.dot(a, b, preferred_element_type=F32)


def _rmsnorm(x, g):
    r = lax.rsqrt(jnp.mean(x * x, axis=-1, keepdims=True) + EPS)
    return x * r * g


def _resident(shape):
    return pl.BlockSpec(shape, lambda *_: (0,) * len(shape), pipeline_mode=pl.Buffered(1))


def _ffn_kernel(xa_ref, xb_ref, pre_ref, wup_ref, wdn_ref, post_ref, *rest, steps_a):
    n_cast = (len(rest) - 2) // 2
    cast_src, (oa_ref, ob_ref), cast_dst = rest[:n_cast], rest[n_cast:n_cast + 2], rest[n_cast + 2:]

    def tile(x_ref, o_ref):
        x = x_ref[...]
        h = _rmsnorm(x, pre_ref[...]).astype(BF16)
        gu = _dot(h, wup_ref[...])
        a = (jax.nn.silu(gu[:, 0:D_FF]) * gu[:, D_FF:]).astype(BF16)
        y = _dot(a, wdn_ref[...])
        o_ref[...] = x + 0.5 * _rmsnorm(y, post_ref[...])

    in_a = pl.program_id(0) < steps_a
    pl.when(in_a)(lambda: tile(xa_ref, oa_ref))
    pl.when(jnp.logical_not(in_a))(lambda: tile(xb_ref, ob_ref))
    for src, dst in zip(cast_src, cast_dst):
        dst[...] = src[...].astype(BF16)


def _ffn(xa, xb, pre_g, w_up, w_down, post_g, tm, to_bf16=()):
    assert xa.shape[0] % tm == 0 and xb.shape[0] % tm == 0 and D_FF % LANES == 0
    steps_a, steps_b = xa.shape[0] // tm, xb.shape[0] // tm
    steps = steps_a + steps_b
    bf16_rows = 2 * SUBLANES
    tile_a = pl.BlockSpec((tm, D), lambda i: (jnp.minimum(i, steps_a - 1), 0))
    tile_b = pl.BlockSpec((tm, D), lambda i: (jnp.maximum(i - steps_a, 0), 0), pipeline_mode=pl.Buffered(1))

    def slab(w):
        blocks = steps_a if w.shape[0] % (steps_a * bf16_rows) == 0 else steps_a // 2
        assert w.shape[0] % (blocks * bf16_rows) == 0
        return pl.BlockSpec((w.shape[0] // blocks, w.shape[1]), lambda i: (jnp.minimum(i, blocks - 1), 0))

    slabs = [slab(w) for w in to_bf16]
    ya, yb, *converted = pl.pallas_call(
        functools.partial(_ffn_kernel, steps_a=steps_a),
        grid=(steps,),
        in_specs=[
            tile_a,
            tile_b,
            _resident((1, D)),
            _resident((D, 2 * D_FF)),
            _resident((D_FF, D)),
            _resident((1, D)),
            *slabs,
        ],
        out_specs=[tile_a, tile_b, *slabs],
        out_shape=[jax.ShapeDtypeStruct(xa.shape, F32), jax.ShapeDtypeStruct(xb.shape, F32)]
                  + [jax.ShapeDtypeStruct(w.shape, BF16) for w in to_bf16],
        compiler_params=pltpu.CompilerParams(
            dimension_semantics=("arbitrary",), vmem_limit_bytes=VMEM_LIMIT),
        name="ffn",
    )(xa, xb, pre_g, w_up, w_down, post_g, *to_bf16)
    return ya, yb, converted


def _rope(z, cos, sin_lo, sin_hi):
    half = ROT_DIM // 2
    outs = []
    for g in range(z.shape[-1] // LANES):
        zg = z[:, g * LANES:(g + 1) * LANES]
        outs.append(zg * cos
                    + pltpu.roll(zg, LANES - half, 1) * sin_lo
                    + pltpu.roll(zg, half, 1) * sin_hi)
    return outs


def _rope_tables(pos):
    half = ROT_DIM // 2
    inv = jnp.exp(-math.log(ROPE_THETA) * jnp.arange(0, ROT_DIM, 2, dtype=F32) / ROT_DIM)
    ang = pos.astype(F32)[:, None] * inv[None, :]
    cos, sin = jnp.cos(ang), jnp.sin(ang)
    n = pos.shape[0]
    pad = jnp.zeros((n, HEAD_DIM - ROT_DIM), F32)
    zero = jnp.zeros((n, half), F32)
    cos_h = jnp.concatenate([cos, cos, pad + 1.0], axis=1)
    lo_h = jnp.concatenate([-sin, zero, pad], axis=1)
    hi_h = jnp.concatenate([zero, sin, pad], axis=1)
    rep = LANES // HEAD_DIM
    return tuple(jnp.tile(t, (1, rep)) for t in (cos_h, lo_h, hi_h))


def _ln_silu(c, g, b):
    mu = jnp.mean(c, axis=-1, keepdims=True)
    var = jnp.mean(jnp.square(c - mu), axis=-1, keepdims=True)
    return jax.nn.silu((c - mu) * lax.rsqrt(var + EPS) * g + b)


def _dup_halves(z, kv):
    src = z[:, (kv // 2) * LANES:(kv // 2 + 1) * LANES]
    other = pltpu.roll(src, HEAD_DIM, 1)
    low = lax.broadcasted_iota(jnp.int32, src.shape, 1) < HEAD_DIM
    return jnp.where(low, src, other) if kv % 2 == 0 else jnp.where(low, other, src)


def _mix_prompt_kernel(x_ref, pre_ref, win_ref, cw_ref, cb_ref, lng_ref, lnb_ref, wco_ref, sinks_ref,
                       wao_ref, wo_ref, post_ref, cos_ref, slo_ref, shi_ref,
                       y_ref, cst_ref, kw_ref, vw_ref,
                       ubuf, cbuf, qbuf, kdup, vdup, gbuf, obuf):
    tq = x_ref.shape[0]
    t = pl.program_id(1)
    n_slab = D // LANES

    @pl.when(t == 0)
    def _():
        ubuf[:, 0:32, :] = jnp.zeros((n_slab, 32, LANES), F32)
        kdup[:, 0:WINDOW, :] = jnp.zeros((N_KV, WINDOW, LANES), BF16)
        vdup[:, 0:WINDOW, :] = jnp.zeros((N_KV, WINDOW, LANES), BF16)

    @pl.when(t > 0)
    def _():
        ubuf[:, 0:32, :] = ubuf[:, tq:tq + 32, :]
        kdup[:, 0:WINDOW, :] = kdup[:, tq:tq + WINDOW, :]
        vdup[:, 0:WINDOW, :] = vdup[:, tq:tq + WINDOW, :]

    x = x_ref[...]
    h = _rmsnorm(x, pre_ref[...]).astype(BF16)

    u = _dot(h, win_ref[:, 0:D]) * jax.nn.sigmoid(_dot(h, win_ref[:, O_GLU_G:O_GLU_G + D]))
    for lg in range(n_slab):
        ubuf[lg, 32:32 + tq, :] = u[:, lg * LANES:(lg + 1) * LANES]
    cst_ref[...] = u[tq - HALO:, :]

    rows_per_unit = CONV_STRIDE * SUBLANES
    for lg in range(n_slab):
        lanes = slice(lg * LANES, (lg + 1) * LANES)
        for base in range(0, tq, rows_per_unit):
            w = {}
            acc = [jnp.broadcast_to(cb_ref[:, lanes], (SUBLANES, LANES))] * CONV_STRIDE
            for off in range(CONV_STRIDE - 1 + CONV_W):
                win = ubuf[lg, pl.ds(base + off + 2, SUBLANES, stride=CONV_STRIDE), :]
                if off < CONV_W:
                    w[off] = jnp.broadcast_to(cw_ref[off:off + 1, lanes], (SUBLANES, LANES))
                for r in range(CONV_STRIDE):
                    if 0 <= off - r < CONV_W:
                        acc[r] = acc[r] + w[off - r] * win
            for r in range(CONV_STRIDE):
                cbuf[lg, pl.ds(base + r, SUBLANES, stride=CONV_STRIDE), :] = acc[r]

    cos, slo, shi = cos_ref[...], slo_ref[...], shi_ref[...]
    chunk = 4 * LANES

    def project(col):
        return _dot(h, win_ref[:, col:col + chunk])

    def kv_chunk(z):
        rk = jnp.concatenate(_rope(z[:, 0:KV_W], cos, slo, shi), axis=-1)
        zv = z[:, KV_W:2 * KV_W]
        for kv in range(N_KV):
            kdup[kv, WINDOW:WINDOW + tq, :] = _dup_halves(rk, kv).astype(BF16)
            vdup[kv, WINDOW:WINDOW + tq, :] = _dup_halves(zv, kv).astype(BF16)
        kw_ref[...] = rk[tq - WINDOW:, :]
        vw_ref[...] = zv[tq - WINDOW:, :]

    def q_chunk(c, z):
        for g, rq in enumerate(_rope(z, cos, slo, shi)):
            col = c * chunk + g * LANES
            qbuf[:, col:col + LANES] = (rq * (HEAD_DIM ** -0.5)).astype(BF16)

    def gate_chunk(c, z):
        gbuf[:, c * chunk:(c + 1) * chunk] = jax.nn.sigmoid(z)

    kv_chunk(project(O_K))
    for c in range(D // chunk):
        q_chunk(c, project(O_Q + c * chunk))
    for c in range(2 * D // chunk):
        gate_chunk(c, project(O_GC + c * chunk))

    c = jnp.concatenate([cbuf[lg] for lg in range(n_slab)], axis=-1)
    cs = _ln_silu(c, lng_ref[...], lnb_ref[...]).astype(BF16)
    conv_out = _dot(cs, wco_ref[...])

    row = lax.broadcasted_iota(jnp.int32, (WINDOW, 2 * WINDOW), 0)
    col = lax.broadcasted_iota(jnp.int32, (WINDOW, 2 * WINDOW), 1)
    band = (col > row) & (col <= row + WINDOW)
    low_q = lax.broadcasted_iota(jnp.int32, (WINDOW, LANES), 1) < HEAD_DIM
    low_kv = lax.broadcasted_iota(jnp.int32, (2 * WINDOW, LANES), 1) < HEAD_DIM
    zero_q = jnp.zeros((WINDOW, LANES), BF16)
    zero_kv = jnp.zeros((2 * WINDOW, LANES), BF16)
    sink_slot = lax.broadcasted_iota(jnp.int32, (WINDOW, LANES), 1) == 0
    not_key0 = lax.broadcasted_iota(jnp.int32, (2 * WINDOW, LANES), 0) > 0
    no_key = jnp.full((WINDOW, LANES), NEG_INF, F32)
    ones_bd = jnp.concatenate([low_kv, ~low_kv], axis=0).astype(F32).astype(BF16)
    for i in range(tq // WINDOW):
        r0 = i * WINDOW
        mask = band if i > 0 else band & (col >= jnp.where(t > 0, 0, WINDOW))
        for kv in range(N_KV):
            kd = kdup[kv, r0:r0 + 2 * WINDOW, :]
            vd = vdup[kv, r0:r0 + 2 * WINDOW, :]
            v_bd = jnp.concatenate([jnp.where(low_kv & not_key0, vd, zero_kv),
                                    jnp.where(~low_kv & not_key0, vd, zero_kv)], axis=0)
            v_aug = jnp.concatenate([v_bd, ones_bd], axis=-1)
            q_rows = []
            for pair in range(GROUP // 2):
                tile_col = (kv * (GROUP // 2) + pair) * LANES
                qp = qbuf[r0:r0 + WINDOW, tile_col:tile_col + LANES]
                q_rows += [jnp.where(low_q, qp, zero_q), jnp.where(low_q, zero_q, qp)]
            s_all = lax.dot_general(jnp.concatenate(q_rows, axis=0), kd, (((1,), (1,)), ((), ())),
                                    preferred_element_type=F32)
            probs = []
            for g in range(GROUP):
                fill = jnp.concatenate([jnp.where(sink_slot, sinks_ref[kv * GROUP + g], no_key), no_key], axis=-1)
                s = jnp.where(mask, s_all[g * WINDOW:(g + 1) * WINDOW], fill)
                probs.append(jnp.exp(s - jnp.max(s, axis=-1, keepdims=True)).astype(BF16))
            for pair in range(GROUP // 2):
                tile_col = (kv * (GROUP // 2) + pair) * LANES
                oa = _dot(jnp.concatenate(probs[2 * pair:2 * pair + 2], axis=-1), v_aug)
                obuf[r0:r0 + WINDOW, tile_col:tile_col + LANES] = (oa[:, 0:LANES] / oa[:, LANES:]).astype(BF16)
    attn_out = _dot(obuf[...], wao_ref[...])

    merged = gbuf[:, 0:D] * conv_out + gbuf[:, D:2 * D] * attn_out
    y = _dot(merged.astype(BF16), wo_ref[...])
    y_ref[...] = x + _rmsnorm(y, post_ref[...])


def _mix_prompt(x, p, tables):
    b, t, _ = x.shape
    tq = MIX_TQ
    assert t % tq == 0 and tq % WINDOW == 0 and tq % (CONV_STRIDE * SUBLANES) == 0
    tile = lambda bi, ti: (bi, ti, 0)
    per_seq = lambda bi, ti: (bi, 0, 0)
    tab = pl.BlockSpec((tq, LANES), lambda bi, ti: (ti, 0))
    return pl.pallas_call(
        _mix_prompt_kernel,
        grid=(b, t // tq),
        in_specs=[
            pl.BlockSpec((None, tq, D), tile),
            _resident((1, D)),
            _resident((D, IN_COLS)),
            _resident((CONV_W, D)),
            _resident((1, D)),
            _resident((1, D)),
            _resident((1, D)),
            _resident((D, D)),
            pl.BlockSpec(memory_space=pltpu.SMEM),
            _resident((D, D)),
            _resident((D, D)),
            _resident((1, D)),
            tab, tab, tab,
        ],
        out_specs=[
            pl.BlockSpec((None, tq, D), tile),
            pl.BlockSpec((None, HALO, D), per_seq),
            pl.BlockSpec((None, WINDOW, KV_W), per_seq),
            pl.BlockSpec((None, WINDOW, KV_W), per_seq),
        ],
        out_shape=[
            jax.ShapeDtypeStruct((b, t, D), F32),
            jax.ShapeDtypeStruct((b, HALO, D), F32),
            jax.ShapeDtypeStruct((b, WINDOW, KV_W), F32),
            jax.ShapeDtypeStruct((b, WINDOW, KV_W), F32),
        ],
        scratch_shapes=[
            pltpu.VMEM((D // LANES, 32 + tq, LANES), F32),
            pltpu.VMEM((D // LANES, tq, LANES), F32),
            pltpu.VMEM((tq, D), BF16),
            pltpu.VMEM((N_KV, WINDOW + tq, LANES), BF16),
            pltpu.VMEM((N_KV, WINDOW + tq, LANES), BF16),
            pltpu.VMEM((tq, 2 * D), F32),
            pltpu.VMEM((tq, D), BF16),
        ],
        compiler_params=pltpu.CompilerParams(
            dimension_semantics=("arbitrary", "arbitrary"), vmem_limit_bytes=VMEM_LIMIT),
        name="mix_prompt",
    )(x, p["mix_pre_g"], p["w_in"], p["conv_dw_w"], p["conv_dw_b"], p["conv_ln_g"], p["conv_ln_b"],
      p["w_conv_out"], p["attn_sinks"], p["w_attn_out"], p["w_out"], p["mix_post_g"], *tables)


def _mix_sample_kernel(x_ref, st_ref, kt_ref, vt_ref, pre_ref, win_ref, cw_ref, cb_ref, lng_ref, lnb_ref,
                       wco_ref, sinkcol_ref, wao_ref, wo_ref, post_ref, cos_ref, slo_ref, shi_ref,
                       y_ref, cst_ref, kto_ref, vto_ref,
                       ubuf, cbuf, qf, qm, sbuf, pbuf, obuf, of):
    nt = x_ref.shape[0]
    s_new = SUBLANES
    sb = nt // s_new
    l_buf = kt_ref.shape[2]
    n_slab = D // LANES

    x = x_ref[...]
    h = _rmsnorm(x, pre_ref[...]).astype(BF16)

    u = _dot(h, win_ref[:, 0:D]) * jax.nn.sigmoid(_dot(h, win_ref[:, O_GLU_G:O_GLU_G + D]))
    for lg in range(n_slab):
        ubuf[lg] = u[:, lg * LANES:(lg + 1) * LANES]

    def conv_row(i, lg):
        if i < HALO:
            return st_ref[i, :, lg * LANES:(lg + 1) * LANES]
        return ubuf[lg, pl.ds(i - HALO, sb, stride=s_new), :]

    for lg in range(n_slab):
        lanes = slice(lg * LANES, (lg + 1) * LANES)
        w = [jnp.broadcast_to(cw_ref[j:j + 1, lanes], (sb, LANES)) for j in range(CONV_W)]
        acc = [jnp.broadcast_to(cb_ref[:, lanes], (sb, LANES))] * s_new
        for i in range(HALO + s_new):
            row_i = conv_row(i, lg)
            for tok in range(s_new):
                if 0 <= i - tok < CONV_W:
                    acc[tok] = acc[tok] + w[i - tok] * row_i
            if i >= s_new:
                cst_ref[i - s_new, :, lanes] = row_i
        for tok in range(s_new):
            cbuf[lg, pl.ds(tok, sb, stride=s_new), :] = acc[tok]
    c = jnp.concatenate([cbuf[lg] for lg in range(n_slab)], axis=-1)
    cs = _ln_silu(c, lng_ref[...], lnb_ref[...]).astype(BF16)
    conv_out = _dot(cs, wco_ref[...])

    cos, slo, shi = cos_ref[...], slo_ref[...], shi_ref[...]
    zq = _dot(h, win_ref[:, O_Q:O_Q + D])
    for g, rq in enumerate(_rope(zq, cos, slo, shi)):
        qf[:, g * LANES:(g + 1) * LANES] = rq * (HEAD_DIM ** -0.5)
    rk = jnp.concatenate(_rope(_dot(h, win_ref[:, O_K:O_K + KV_W]), cos, slo, shi), axis=-1)
    zv = _dot(h, win_ref[:, O_V:O_V + KV_W])
    k_new, v_new = rk.astype(BF16), zv.astype(BF16)
    k_new_t, v_new_t = rk.T, zv.T

    n_rows = N_HEADS * s_new
    nt_dims = (((1,), (1,)), ((), ()))
    low_half = lax.broadcasted_iota(jnp.int32, (nt, LANES), 1) < HEAD_DIM
    zeros = jnp.zeros((sb, s_new, LANES), F32)

    for head in range(N_HEADS):
        kv = head // GROUP
        src = qf[:, (head // 2) * LANES:(head // 2 + 1) * LANES]
        if head % 2 != kv % 2:
            src = pltpu.roll(src, HEAD_DIM, 1)
        piece = jnp.where(low_half if kv % 2 == 0 else ~low_half, src, 0.0).reshape(sb, s_new, LANES)
        qm[:, head * s_new:(head + 1) * s_new, :] = jnp.concatenate(
            [piece, zeros] if kv // 2 == 0 else [zeros, piece], axis=-1)

    keep_old = lax.broadcasted_iota(jnp.int32, (KV_W, LANES), 1) < l_buf - s_new
    shift = l_buf - s_new

    def new_columns(z_t, b):
        amount = (shift - b * s_new) % LANES
        return pltpu.roll(z_t, amount, 1) if amount else z_t

    for b in range(sb):
        qb = qm[b].astype(BF16)
        kt = kt_ref[b]
        sbuf[b] = jnp.concatenate([_dot(qb, kt.astype(BF16)),
                                   lax.dot_general(qb, k_new, nt_dims, preferred_element_type=F32)], axis=-1)
        kto_ref[b] = jnp.where(keep_old, pltpu.roll(kt, shift, 1), new_columns(k_new_t, b))

    shape3 = (sb, n_rows, l_buf + nt)
    seq = lax.broadcasted_iota(jnp.int32, shape3, 0)
    tok = jnp.bitwise_and(lax.broadcasted_iota(jnp.int32, shape3, 1), s_new - 1)
    col = lax.broadcasted_iota(jnp.int32, shape3, 2)
    first_new = l_buf + seq * s_new
    mask = ((col < l_buf) & (col > tok + l_buf - WINDOW)) | ((col >= first_new) & (col <= first_new + tok))
    fill = jnp.where(col == 0, sinkcol_ref[...].reshape(1, n_rows, 1), NEG_INF)
    s = jnp.where(mask, sbuf[...], fill)
    pbuf[...] = jnp.exp(s - jnp.max(s, axis=-1, keepdims=True)).astype(BF16)

    ones = jnp.ones((LANES, LANES), BF16)
    not_key0 = lax.broadcasted_iota(jnp.int32, (KV_W, l_buf), 1) > 0
    v_new_aug = jnp.concatenate([v_new, ones], axis=1)
    for b in range(sb):
        vt = vt_ref[b]
        pb = pbuf[b]
        vt_aug = jnp.concatenate([jnp.where(not_key0, vt, 0.0).astype(BF16), ones], axis=0)
        oa = (lax.dot_general(pb[:, 0:l_buf], vt_aug, nt_dims, preferred_element_type=F32)
              + _dot(pb[:, l_buf:], v_new_aug))
        den = oa[:, KV_W:]
        obuf[b] = oa[:, 0:KV_W] / jnp.concatenate([den] * (KV_W // LANES), axis=-1)
        vto_ref[b] = jnp.where(keep_old, pltpu.roll(vt, shift, 1), new_columns(v_new_t, b))

    for pair in range(N_HEADS // 2):
        kv = (2 * pair) // GROUP
        halves = []
        for head in (2 * pair, 2 * pair + 1):
            blk = obuf[:, head * s_new:(head + 1) * s_new, (kv // 2) * LANES:(kv // 2 + 1) * LANES]
            blk = blk.reshape(nt, LANES)
            halves.append(pltpu.roll(blk, HEAD_DIM, 1) if head % 2 != kv % 2 else blk)
        of[:, pair * LANES:(pair + 1) * LANES] = jnp.where(low_half, halves[0], halves[1])
    attn_out = _dot(of[...].astype(BF16), wao_ref[...])

    g_conv = jax.nn.sigmoid(_dot(h, win_ref[:, O_GC:O_GC + D]))
    g_attn = jax.nn.sigmoid(_dot(h, win_ref[:, O_GA:O_GA + D]))
    y = _dot((g_conv * conv_out + g_attn * attn_out).astype(BF16), wo_ref[...])
    y_ref[...] = x + _rmsnorm(y, post_ref[...])


def _mix_sample(x2d, state_t, kt, vt, p, tables):
    nb, _, l_buf = kt.shape
    s_new = x2d.shape[0] // nb
    sb = SAMPLE_SB
    nt = sb * s_new
    assert s_new == SUBLANES and l_buf == LANES == WINDOW and nt == LANES and nb % sb == 0
    sink_col = jnp.repeat(p["attn_sinks"], s_new).reshape(N_HEADS * s_new, 1)
    rows = lambda i: (i, 0)
    seqs = lambda i: (i, 0, 0)
    taps = lambda i: (0, i, 0)
    tab = pl.BlockSpec((nt, LANES), lambda i: (0, 0), pipeline_mode=pl.Buffered(1))
    return pl.pallas_call(
        _mix_sample_kernel,
        grid=(nb // sb,),
        in_specs=[
            pl.BlockSpec((nt, D), rows),
            pl.BlockSpec((HALO, sb, D), taps),
            pl.BlockSpec((sb, KV_W, l_buf), seqs),
            pl.BlockSpec((sb, KV_W, l_buf), seqs),
            _resident((1, D)),
            _resident((D, IN_COLS)),
            _resident((CONV_W, D)),
            _resident((1, D)),
            _resident((1, D)),
            _resident((1, D)),
            _resident((D, D)),
            _resident((N_HEADS * s_new, 1)),
            _resident((D, D)),
            _resident((D, D)),
            _resident((1, D)),
            tab, tab, tab,
        ],
        out_specs=[
            pl.BlockSpec((nt, D), rows),
            pl.BlockSpec((HALO, sb, D), taps),
            pl.BlockSpec((sb, KV_W, l_buf), seqs),
            pl.BlockSpec((sb, KV_W, l_buf), seqs),
        ],
        out_shape=[
            jax.ShapeDtypeStruct(x2d.shape, F32),
            jax.ShapeDtypeStruct((HALO, nb, D), F32),
            jax.ShapeDtypeStruct((nb, KV_W, l_buf), F32),
            jax.ShapeDtypeStruct((nb, KV_W, l_buf), F32),
        ],
        scratch_shapes=[
            pltpu.VMEM((D // LANES, nt, LANES), F32),
            pltpu.VMEM((D // LANES, nt, LANES), F32),
            pltpu.VMEM((nt, D), F32),
            pltpu.VMEM((sb, N_HEADS * s_new, KV_W), F32),
            pltpu.VMEM((sb, N_HEADS * s_new, l_buf + nt), F32),
            pltpu.VMEM((sb, N_HEADS * s_new, l_buf + nt), BF16),
            pltpu.VMEM((sb, N_HEADS * s_new, KV_W), F32),
            pltpu.VMEM((nt, D), F32),
        ],
        compiler_params=pltpu.CompilerParams(
            dimension_semantics=("arbitrary",), vmem_limit_bytes=VMEM_LIMIT),
        name="mix_sample",
    )(x2d, state_t, kt, vt, p["mix_pre_g"], p["w_in"], p["conv_dw_w"], p["conv_dw_b"], p["conv_ln_g"],
      p["conv_ln_b"], p["w_conv_out"], sink_col, p["w_attn_out"], p["w_out"], p["mix_post_g"], *tables)


def kernel(x_prompt, x_sample, state_conv, cache_k_win, cache_v_win, ffn1_pre_g, ffn1_w_up, ffn1_w_down, ffn1_post_g, mix_pre_g, w_in, conv_dw_w, conv_dw_b, conv_ln_g, conv_ln_b, w_conv_out, attn_sinks, w_attn_out, w_out, mix_post_g, ffn2_pre_g, ffn2_w_up, ffn2_w_down, ffn2_post_g):
    depth = w_in.shape[0]
    b, t, _ = x_prompt.shape
    nb, s_new, _ = x_sample.shape
    l_buf = cache_k_win.shape[2]
    later_weights = dict(w_in=w_in, w_conv_out=w_conv_out, w_attn_out=w_attn_out, w_out=w_out,
                         ffn2_w_up=ffn2_w_up, ffn2_w_down=ffn2_w_down)
    other = dict(ffn1_pre_g=ffn1_pre_g, ffn1_post_g=ffn1_post_g, mix_pre_g=mix_pre_g, conv_dw_w=conv_dw_w,
                 conv_dw_b=conv_dw_b, conv_ln_g=conv_ln_g, conv_ln_b=conv_ln_b, attn_sinks=attn_sinks,
                 mix_post_g=mix_post_g, ffn2_pre_g=ffn2_pre_g, ffn2_post_g=ffn2_post_g)
    tab_p = _rope_tables(jnp.arange(t))
    tab_s = _rope_tables(jnp.tile(PAST_LEN + jnp.arange(s_new), SAMPLE_SB))

    hp = x_prompt.reshape(b * t, D)
    hs = x_sample.reshape(nb * s_new, D)
    outs = [[] for _ in range(6)]
    for l in range(depth):
        p = {"ffn1_w_up": ffn1_w_up[l].astype(BF16), "ffn1_w_down": ffn1_w_down[l].astype(BF16)}
        for n, w in other.items():
            p[n] = w[l] if w[l].ndim == 2 or n == "attn_sinks" else w[l].reshape(1, -1)

        hp, hs, converted = _ffn(hp, hs, p["ffn1_pre_g"], p["ffn1_w_up"], p["ffn1_w_down"], p["ffn1_post_g"],
                                 FFN_TM_FIRST, tuple(w[l] for w in later_weights.values()))
        p.update(zip(later_weights, converted))
        hp, c1, k1, v1 = _mix_prompt(hp.reshape(b, t, D), p, tab_p)
        rows_last = lambda w: jnp.transpose(w, (0, 2, 3, 1)).reshape(nb, KV_W, l_buf)
        rows_back = lambda w: jnp.transpose(w.reshape(nb, N_KV, HEAD_DIM, l_buf), (0, 3, 1, 2))
        hs, c2, k2, v2 = _mix_sample(hs, jnp.transpose(state_conv[l], (1, 0, 2)),
                                     rows_last(cache_k_win[l]), rows_last(cache_v_win[l]), p, tab_s)
        hp, hs, _ = _ffn(hp.reshape(b * t, D), hs, p["ffn2_pre_g"], p["ffn2_w_up"], p["ffn2_w_down"],
                         p["ffn2_post_g"], FFN_TM)

        kv_shape = lambda n: (n, -1, N_KV, HEAD_DIM)
        for acc, val in zip(outs, (c1, k1.reshape(kv_shape(b)), v1.reshape(kv_shape(b)),
                                   jnp.transpose(c2, (1, 0, 2)), rows_back(k2), rows_back(v2))):
            acc.append(val)
    return (hp.reshape(b, t, D), hs.reshape(nb, s_new, D), *(jnp.stack(o) for o in outs))
```

```python
import math

import jax
import jax.numpy as jnp
from jax import lax
from jax.experimental import pallas as pl
from jax.experimental.pallas import tpu as pltpu

D = 1024
N_HEADS = 16
HEAD_DIM = 64
N_KV = 4
GROUP = 4
ROT_DIM = 16
ROPE_THETA = 500000.0
WINDOW = 128
CONV_W = 31
HALO = CONV_W - 1
D_FF = 2816
KV_W = N_KV * HEAD_DIM
EPS = 1e-6
NEG_INF = -1e30
PAST_LEN = 8192
O_GLU_G = D
O_Q = 2 * D
O_K = O_Q + D
O_V = O_K + KV_W
O_GC = O_V + KV_W
O_GA = O_GC + D
IN_COLS = O_GA + D

LANES = 128
SUBLANES = 8
VMEM_LIMIT = 56 * 1024 * 1024

FFN_TM = 1024
FFN_TM_SMALL = 512
MIX_TQ = 512
CONV_STRIDE = 4
SAMPLE_SB = 16

F32 = jnp.float32
BF16 = jnp.bfloat16


def _dot(a, b):
    return jnp.dot(a, b, preferred_element_type=F32)


def _rmsnorm(x, g):
    r = lax.rsqrt(jnp.mean(x * x, axis=-1, keepdims=True) + EPS)
    return x * r * g


def _resident(shape):
    return pl.BlockSpec(shape, lambda *_: (0,) * len(shape), pipeline_mode=pl.Buffered(1))


def _ffn_kernel(x_ref, pre_ref, wup_ref, wdn_ref, post_ref, *rest):
    n_cast = len(rest) // 2
    cast_src, o_ref, cast_dst = rest[:n_cast], rest[n_cast], rest[n_cast + 1:]
    x = x_ref[...]
    h = _rmsnorm(x, pre_ref[...]).astype(BF16)
    gu = _dot(h, wup_ref[...])
    a = (jax.nn.silu(gu[:, 0:D_FF]) * gu[:, D_FF:]).astype(BF16)
    y = _dot(a, wdn_ref[...])
    o_ref[...] = x + 0.5 * _rmsnorm(y, post_ref[...])
    for src, dst in zip(cast_src, cast_dst):
        dst[...] = src[...].astype(BF16)


def _slab_spec(w, steps, step_of=lambda i: i):
    bf16_rows = 2 * SUBLANES
    blocks = steps if w.shape[0] % (steps * bf16_rows) == 0 else steps // 2
    assert w.shape[0] % (blocks * bf16_rows) == 0
    return pl.BlockSpec((w.shape[0] // blocks, w.shape[1]),
                        lambda *idx: (jnp.minimum(step_of(*idx), blocks - 1), 0))


def _ffn(x2d, pre_g, w_up, w_down, post_g, tm, to_bf16=()):
    n = x2d.shape[0]
    steps = n // tm
    assert n % tm == 0 and D_FF % LANES == 0
    tile = pl.BlockSpec((tm, D), lambda i: (i, 0))
    slabs = [_slab_spec(w, steps) for w in to_bf16]
    y, *converted = pl.pallas_call(
        _ffn_kernel,
        grid=(steps,),
        in_specs=[
            tile,
            _resident((1, D)),
            _resident((D, 2 * D_FF)),
            _resident((D_FF, D)),
            _resident((1, D)),
            *slabs,
        ],
        out_specs=[tile, *slabs],
        out_shape=[jax.ShapeDtypeStruct((n, D), F32)] + [jax.ShapeDtypeStruct(w.shape, BF16) for w in to_bf16],
        compiler_params=pltpu.CompilerParams(
            dimension_semantics=("arbitrary",), vmem_limit_bytes=VMEM_LIMIT),
        name="ffn",
    )(x2d, pre_g, w_up, w_down, post_g, *to_bf16)
    return y, converted


def _rope(z, cos, sin_lo, sin_hi):
    half = ROT_DIM // 2
    outs = []
    for g in range(z.shape[-1] // LANES):
        zg = z[:, g * LANES:(g + 1) * LANES]
        outs.append(zg * cos
                    + pltpu.roll(zg, LANES - half, 1) * sin_lo
                    + pltpu.roll(zg, half, 1) * sin_hi)
    return outs


def _rope_tables(pos):
    half = ROT_DIM // 2
    inv = jnp.exp(-math.log(ROPE_THETA) * jnp.arange(0, ROT_DIM, 2, dtype=F32) / ROT_DIM)
    ang = pos.astype(F32)[:, None] * inv[None, :]
    cos, sin = jnp.cos(ang), jnp.sin(ang)
    n = pos.shape[0]
    pad = jnp.zeros((n, HEAD_DIM - ROT_DIM), F32)
    zero = jnp.zeros((n, half), F32)
    cos_h = jnp.concatenate([cos, cos, pad + 1.0], axis=1)
    lo_h = jnp.concatenate([-sin, zero, pad], axis=1)
    hi_h = jnp.concatenate([zero, sin, pad], axis=1)
    rep = LANES // HEAD_DIM
    return tuple(jnp.tile(t, (1, rep)) for t in (cos_h, lo_h, hi_h))


def _ln_silu(c, g, b):
    mu = jnp.mean(c, axis=-1, keepdims=True)
    var = jnp.mean(jnp.square(c - mu), axis=-1, keepdims=True)
    return jax.nn.silu((c - mu) * lax.rsqrt(var + EPS) * g + b)


def _dup_halves(z, kv):
    src = z[:, (kv // 2) * LANES:(kv // 2 + 1) * LANES]
    other = pltpu.roll(src, HEAD_DIM, 1)
    low = lax.broadcasted_iota(jnp.int32, src.shape, 1) < HEAD_DIM
    return jnp.where(low, src, other) if kv % 2 == 0 else jnp.where(low, other, src)


def _mix_prompt_kernel(x_ref, pre_ref, win_ref, cw_ref, cb_ref, lng_ref, lnb_ref, wco_ref, sinks_ref,
                       wao_ref, wo_ref, post_ref, cos_ref, slo_ref, shi_ref, *rest):
    n_cast = (len(rest) - 4 - 7) // 2
    cast_src, cast_dst = rest[:n_cast], rest[n_cast + 4:2 * n_cast + 4]
    y_ref, cst_ref, kw_ref, vw_ref = rest[n_cast:n_cast + 4]
    ubuf, cbuf, qbuf, kdup, vdup, gbuf, obuf = rest[2 * n_cast + 4:]
    tq = x_ref.shape[0]
    t = pl.program_id(1)
    n_slab = D // LANES

    @pl.when(t == 0)
    def _():
        ubuf[:, 0:32, :] = jnp.zeros((n_slab, 32, LANES), F32)
        kdup[:, 0:WINDOW, :] = jnp.zeros((N_KV, WINDOW, LANES), BF16)
        vdup[:, 0:WINDOW, :] = jnp.zeros((N_KV, WINDOW, LANES), BF16)

    @pl.when(t > 0)
    def _():
        ubuf[:, 0:32, :] = ubuf[:, tq:tq + 32, :]
        kdup[:, 0:WINDOW, :] = kdup[:, tq:tq + WINDOW, :]
        vdup[:, 0:WINDOW, :] = vdup[:, tq:tq + WINDOW, :]

    x = x_ref[...]
    h = _rmsnorm(x, pre_ref[...]).astype(BF16)

    u = _dot(h, win_ref[:, 0:D]) * jax.nn.sigmoid(_dot(h, win_ref[:, O_GLU_G:O_GLU_G + D]))
    for lg in range(n_slab):
        ubuf[lg, 32:32 + tq, :] = u[:, lg * LANES:(lg + 1) * LANES]
    cst_ref[...] = u[tq - HALO:, :]

    rows_per_unit = CONV_STRIDE * SUBLANES
    for lg in range(n_slab):
        lanes = slice(lg * LANES, (lg + 1) * LANES)
        for base in range(0, tq, rows_per_unit):
            w = {}
            acc = [jnp.broadcast_to(cb_ref[:, lanes], (SUBLANES, LANES))] * CONV_STRIDE
            for off in range(CONV_STRIDE - 1 + CONV_W):
                win = ubuf[lg, pl.ds(base + off + 2, SUBLANES, stride=CONV_STRIDE), :]
                if off < CONV_W:
                    w[off] = jnp.broadcast_to(cw_ref[off:off + 1, lanes], (SUBLANES, LANES))
                for r in range(CONV_STRIDE):
                    if 0 <= off - r < CONV_W:
                        acc[r] = acc[r] + w[off - r] * win
            for r in range(CONV_STRIDE):
                cbuf[lg, pl.ds(base + r, SUBLANES, stride=CONV_STRIDE), :] = acc[r]

    cos, slo, shi = cos_ref[...], slo_ref[...], shi_ref[...]
    chunk = 4 * LANES

    def project(col):
        return _dot(h, win_ref[:, col:col + chunk])

    def kv_chunk(z):
        rk = jnp.concatenate(_rope(z[:, 0:KV_W], cos, slo, shi), axis=-1)
        zv = z[:, KV_W:2 * KV_W]
        for kv in range(N_KV):
            kdup[kv, WINDOW:WINDOW + tq, :] = _dup_halves(rk, kv).astype(BF16)
            vdup[kv, WINDOW:WINDOW + tq, :] = _dup_halves(zv, kv).astype(BF16)
        kw_ref[...] = rk[tq - WINDOW:, :]
        vw_ref[...] = zv[tq - WINDOW:, :]

    def q_chunk(c, z):
        for g, rq in enumerate(_rope(z, cos, slo, shi)):
            col = c * chunk + g * LANES
            qbuf[:, col:col + LANES] = (rq * (HEAD_DIM ** -0.5)).astype(BF16)

    def gate_chunk(c, z):
        gbuf[:, c * chunk:(c + 1) * chunk] = jax.nn.sigmoid(z)

    kv_chunk(project(O_K))
    for c in range(D // chunk):
        q_chunk(c, project(O_Q + c * chunk))
    for c in range(2 * D // chunk):
        gate_chunk(c, project(O_GC + c * chunk))

    c = jnp.concatenate([cbuf[lg] for lg in range(n_slab)], axis=-1)
    cs = _ln_silu(c, lng_ref[...], lnb_ref[...]).astype(BF16)
    conv_out = _dot(cs, wco_ref[...])

    row = lax.broadcasted_iota(jnp.int32, (WINDOW, 2 * WINDOW), 0)
    col = lax.broadcasted_iota(jnp.int32, (WINDOW, 2 * WINDOW), 1)
    band = (col > row) & (col <= row + WINDOW)
    low_q = lax.broadcasted_iota(jnp.int32, (WINDOW, LANES), 1) < HEAD_DIM
    low_kv = lax.broadcasted_iota(jnp.int32, (2 * WINDOW, LANES), 1) < HEAD_DIM
    zero_q = jnp.zeros((WINDOW, LANES), BF16)
    zero_kv = jnp.zeros((2 * WINDOW, LANES), BF16)
    sink_slot = lax.broadcasted_iota(jnp.int32, (WINDOW, LANES), 1) == 0
    not_key0 = lax.broadcasted_iota(jnp.int32, (2 * WINDOW, LANES), 0) > 0
    no_key = jnp.full((WINDOW, LANES), NEG_INF, F32)
    ones_bd = jnp.concatenate([low_kv, ~low_kv], axis=0).astype(F32).astype(BF16)
    for i in range(tq // WINDOW):
        r0 = i * WINDOW
        mask = band if i > 0 else band & (col >= jnp.where(t > 0, 0, WINDOW))
        for kv in range(N_KV):
            kd = kdup[kv, r0:r0 + 2 * WINDOW, :]
            vd = vdup[kv, r0:r0 + 2 * WINDOW, :]
            v_bd = jnp.concatenate([jnp.where(low_kv & not_key0, vd, zero_kv),
                                    jnp.where(~low_kv & not_key0, vd, zero_kv)], axis=0)
            v_aug = jnp.concatenate([v_bd, ones_bd], axis=-1)
            q_rows = []
            for pair in range(GROUP // 2):
                tile_col = (kv * (GROUP // 2) + pair) * LANES
                qp = qbuf[r0:r0 + WINDOW, tile_col:tile_col + LANES]
                q_rows += [jnp.where(low_q, qp, zero_q), jnp.where(low_q, zero_q, qp)]
            s_all = lax.dot_general(jnp.concatenate(q_rows, axis=0), kd, (((1,), (1,)), ((), ())),
                                    preferred_element_type=F32)
            probs = []
            for g in range(GROUP):
                fill = jnp.concatenate([jnp.where(sink_slot, sinks_ref[kv * GROUP + g], no_key), no_key], axis=-1)
                s = jnp.where(mask, s_all[g * WINDOW:(g + 1) * WINDOW], fill)
                probs.append(jnp.exp(s - jnp.max(s, axis=-1, keepdims=True)).astype(BF16))
            for pair in range(GROUP // 2):
                tile_col = (kv * (GROUP // 2) + pair) * LANES
                oa = _dot(jnp.concatenate(probs[2 * pair:2 * pair + 2], axis=-1), v_aug)
                obuf[r0:r0 + WINDOW, tile_col:tile_col + LANES] = (oa[:, 0:LANES] / oa[:, LANES:]).astype(BF16)
    attn_out = _dot(obuf[...], wao_ref[...])

    merged = gbuf[:, 0:D] * conv_out + gbuf[:, D:2 * D] * attn_out
    y = _dot(merged.astype(BF16), wo_ref[...])
    y_ref[...] = x + _rmsnorm(y, post_ref[...])
    for src, dst in zip(cast_src, cast_dst):
        dst[...] = src[...].astype(BF16)


def _mix_prompt(x, p, tables, to_bf16=()):
    b, t, _ = x.shape
    tq = MIX_TQ
    assert t % tq == 0 and tq % WINDOW == 0 and tq % (CONV_STRIDE * SUBLANES) == 0
    n_t = t // tq
    tile = lambda bi, ti: (bi, ti, 0)
    per_seq = lambda bi, ti: (bi, 0, 0)
    tab = pl.BlockSpec((tq, LANES), lambda bi, ti: (ti, 0))
    slabs = [_slab_spec(w, b * n_t, lambda bi, ti: bi * n_t + ti) for w in to_bf16]
    y, cst, kw, vw, *converted = pl.pallas_call(
        _mix_prompt_kernel,
        grid=(b, t // tq),
        in_specs=[
            pl.BlockSpec((None, tq, D), tile),
            _resident((1, D)),
            _resident((D, IN_COLS)),
            _resident((CONV_W, D)),
            _resident((1, D)),
            _resident((1, D)),
            _resident((1, D)),
            _resident((D, D)),
            pl.BlockSpec(memory_space=pltpu.SMEM),
            _resident((D, D)),
            _resident((D, D)),
            _resident((1, D)),
            tab, tab, tab,
            *slabs,
        ],
        out_specs=[
            pl.BlockSpec((None, tq, D), tile),
            pl.BlockSpec((None, HALO, D), per_seq),
            pl.BlockSpec((None, WINDOW, KV_W), per_seq),
            pl.BlockSpec((None, WINDOW, KV_W), per_seq),
            *slabs,
        ],
        out_shape=[
            jax.ShapeDtypeStruct((b, t, D), F32),
            jax.ShapeDtypeStruct((b, HALO, D), F32),
            jax.ShapeDtypeStruct((b, WINDOW, KV_W), F32),
            jax.ShapeDtypeStruct((b, WINDOW, KV_W), F32),
            *(jax.ShapeDtypeStruct(w.shape, BF16) for w in to_bf16),
        ],
        scratch_shapes=[
            pltpu.VMEM((D // LANES, 32 + tq, LANES), F32),
            pltpu.VMEM((D // LANES, tq, LANES), F32),
            pltpu.VMEM((tq, D), BF16),
            pltpu.VMEM((N_KV, WINDOW + tq, LANES), BF16),
            pltpu.VMEM((N_KV, WINDOW + tq, LANES), BF16),
            pltpu.VMEM((tq, 2 * D), F32),
            pltpu.VMEM((tq, D), BF16),
        ],
        compiler_params=pltpu.CompilerParams(
            dimension_semantics=("arbitrary", "arbitrary"), vmem_limit_bytes=VMEM_LIMIT),
        name="mix_prompt",
    )(x, p["mix_pre_g"], p["w_in"], p["conv_dw_w"], p["conv_dw_b"], p["conv_ln_g"], p["conv_ln_b"],
      p["w_conv_out"], p["attn_sinks"], p["w_attn_out"], p["w_out"], p["mix_post_g"], *tables, *to_bf16)
    return y, cst, kw, vw, converted


def _mix_sample_kernel(x_ref, st_ref, kt_ref, vt_ref, pre_ref, win_ref, cw_ref, cb_ref, lng_ref, lnb_ref,
                       wco_ref, sinkcol_ref, wao_ref, wo_ref, post_ref, cos_ref, slo_ref, shi_ref,
                       y_ref, cst_ref, kto_ref, vto_ref,
                       ubuf, cbuf, qf, qm, sbuf, pbuf, obuf, of):
    nt = x_ref.shape[0]
    s_new = SUBLANES
    sb = nt // s_new
    l_buf = kt_ref.shape[2]
    n_slab = D // LANES

    x = x_ref[...]
    h = _rmsnorm(x, pre_ref[...]).astype(BF16)

    u = _dot(h, win_ref[:, 0:D]) * jax.nn.sigmoid(_dot(h, win_ref[:, O_GLU_G:O_GLU_G + D]))
    for lg in range(n_slab):
        ubuf[lg] = u[:, lg * LANES:(lg + 1) * LANES]

    def conv_row(i, lg):
        if i < HALO:
            return st_ref[i, :, lg * LANES:(lg + 1) * LANES]
        return ubuf[lg, pl.ds(i - HALO, sb, stride=s_new), :]

    for lg in range(n_slab):
        lanes = slice(lg * LANES, (lg + 1) * LANES)
        w = [jnp.broadcast_to(cw_ref[j:j + 1, lanes], (sb, LANES)) for j in range(CONV_W)]
        acc = [jnp.broadcast_to(cb_ref[:, lanes], (sb, LANES))] * s_new
        for i in range(HALO + s_new):
            row_i = conv_row(i, lg)
            for tok in range(s_new):
                if 0 <= i - tok < CONV_W:
                    acc[tok] = acc[tok] + w[i - tok] * row_i
            if i >= s_new:
                cst_ref[i - s_new, :, lanes] = row_i
        for tok in range(s_new):
            cbuf[lg, pl.ds(tok, sb, stride=s_new), :] = acc[tok]
    c = jnp.concatenate([cbuf[lg] for lg in range(n_slab)], axis=-1)
    cs = _ln_silu(c, lng_ref[...], lnb_ref[...]).astype(BF16)
    conv_out = _dot(cs, wco_ref[...])

    cos, slo, shi = cos_ref[...], slo_ref[...], shi_ref[...]
    zq = _dot(h, win_ref[:, O_Q:O_Q + D])
    for g, rq in enumerate(_rope(zq, cos, slo, shi)):
        qf[:, g * LANES:(g + 1) * LANES] = rq * (HEAD_DIM ** -0.5)
    rk = jnp.concatenate(_rope(_dot(h, win_ref[:, O_K:O_K + KV_W]), cos, slo, shi), axis=-1)
    zv = _dot(h, win_ref[:, O_V:O_V + KV_W])
    k_new, v_new = rk.astype(BF16), zv.astype(BF16)
    k_new_t, v_new_t = rk.T, zv.T

    n_rows = N_HEADS * s_new
    nt_dims = (((1,), (1,)), ((), ()))
    low_half = lax.broadcasted_iota(jnp.int32, (nt, LANES), 1) < HEAD_DIM
    zeros = jnp.zeros((sb, s_new, LANES), F32)

    for head in range(N_HEADS):
        kv = head // GROUP
        src = qf[:, (head // 2) * LANES:(head // 2 + 1) * LANES]
        if head % 2 != kv % 2:
            src = pltpu.roll(src, HEAD_DIM, 1)
        piece = jnp.where(low_half if kv % 2 == 0 else ~low_half, src, 0.0).reshape(sb, s_new, LANES)
        qm[:, head * s_new:(head + 1) * s_new, :] = jnp.concatenate(
            [piece, zeros] if kv // 2 == 0 else [zeros, piece], axis=-1)

    keep_old = lax.broadcasted_iota(jnp.int32, (KV_W, LANES), 1) < l_buf - s_new
    shift = l_buf - s_new

    def new_columns(z_t, b):
        amount = (shift - b * s_new) % LANES
        return pltpu.roll(z_t, amount, 1) if amount else z_t

    for b in range(sb):
        qb = qm[b].astype(BF16)
        kt = kt_ref[b]
        sbuf[b] = jnp.concatenate([_dot(qb, kt.astype(BF16)),
                                   lax.dot_general(qb, k_new, nt_dims, preferred_element_type=F32)], axis=-1)
        kto_ref[b] = jnp.where(keep_old, pltpu.roll(kt, shift, 1), new_columns(k_new_t, b))

    shape3 = (sb, n_rows, l_buf + nt)
    seq = lax.broadcasted_iota(jnp.int32, shape3, 0)
    tok = jnp.bitwise_and(lax.broadcasted_iota(jnp.int32, shape3, 1), s_new - 1)
    col = lax.broadcasted_iota(jnp.int32, shape3, 2)
    first_new = l_buf + seq * s_new
    mask = ((col < l_buf) & (col > tok + l_buf - WINDOW)) | ((col >= first_new) & (col <= first_new + tok))
    fill = jnp.where(col == 0, sinkcol_ref[...].reshape(1, n_rows, 1), NEG_INF)
    s = jnp.where(mask, sbuf[...], fill)
    pbuf[...] = jnp.exp(s - jnp.max(s, axis=-1, keepdims=True)).astype(BF16)

    ones = jnp.ones((LANES, LANES), BF16)
    not_key0 = lax.broadcasted_iota(jnp.int32, (KV_W, l_buf), 1) > 0
    v_new_aug = jnp.concatenate([v_new, ones], axis=1)
    for b in range(sb):
        vt = vt_ref[b]
        pb = pbuf[b]
        vt_aug = jnp.concatenate([jnp.where(not_key0, vt, 0.0).astype(BF16), ones], axis=0)
        oa = (lax.dot_general(pb[:, 0:l_buf], vt_aug, nt_dims, preferred_element_type=F32)
              + _dot(pb[:, l_buf:], v_new_aug))
        den = oa[:, KV_W:]
        obuf[b] = oa[:, 0:KV_W] / jnp.concatenate([den] * (KV_W // LANES), axis=-1)
        vto_ref[b] = jnp.where(keep_old, pltpu.roll(vt, shift, 1), new_columns(v_new_t, b))

    for pair in range(N_HEADS // 2):
        kv = (2 * pair) // GROUP
        halves = []
        for head in (2 * pair, 2 * pair + 1):
            blk = obuf[:, head * s_new:(head + 1) * s_new, (kv // 2) * LANES:(kv // 2 + 1) * LANES]
            blk = blk.reshape(nt, LANES)
            halves.append(pltpu.roll(blk, HEAD_DIM, 1) if head % 2 != kv % 2 else blk)
        of[:, pair * LANES:(pair + 1) * LANES] = jnp.where(low_half, halves[0], halves[1])
    attn_out = _dot(of[...].astype(BF16), wao_ref[...])

    g_conv = jax.nn.sigmoid(_dot(h, win_ref[:, O_GC:O_GC + D]))
    g_attn = jax.nn.sigmoid(_dot(h, win_ref[:, O_GA:O_GA + D]))
    y = _dot((g_conv * conv_out + g_attn * attn_out).astype(BF16), wo_ref[...])
    y_ref[...] = x + _rmsnorm(y, post_ref[...])


def _mix_sample(x2d, state_t, kt, vt, p, tables):
    nb, _, l_buf = kt.shape
    s_new = x2d.shape[0] // nb
    sb = SAMPLE_SB
    nt = sb * s_new
    assert s_new == SUBLANES and l_buf == LANES == WINDOW and nt == LANES and nb % sb == 0
    sink_col = jnp.repeat(p["attn_sinks"], s_new).reshape(N_HEADS * s_new, 1)
    rows = lambda i: (i, 0)
    seqs = lambda i: (i, 0, 0)
    taps = lambda i: (0, i, 0)
    tab = pl.BlockSpec((nt, LANES), lambda i: (0, 0), pipeline_mode=pl.Buffered(1))
    return pl.pallas_call(
        _mix_sample_kernel,
        grid=(nb // sb,),
        in_specs=[
            pl.BlockSpec((nt, D), rows),
            pl.BlockSpec((HALO, sb, D), taps),
            pl.BlockSpec((sb, KV_W, l_buf), seqs),
            pl.BlockSpec((sb, KV_W, l_buf), seqs),
            _resident((1, D)),
            _resident((D, IN_COLS)),
            _resident((CONV_W, D)),
            _resident((1, D)),
            _resident((1, D)),
            _resident((1, D)),
            _resident((D, D)),
            _resident((N_HEADS * s_new, 1)),
            _resident((D, D)),
            _resident((D, D)),
            _resident((1, D)),
            tab, tab, tab,
        ],
        out_specs=[
            pl.BlockSpec((nt, D), rows),
            pl.BlockSpec((HALO, sb, D), taps),
            pl.BlockSpec((sb, KV_W, l_buf), seqs),
            pl.BlockSpec((sb, KV_W, l_buf), seqs),
        ],
        out_shape=[
            jax.ShapeDtypeStruct(x2d.shape, F32),
            jax.ShapeDtypeStruct((HALO, nb, D), F32),
            jax.ShapeDtypeStruct((nb, KV_W, l_buf), F32),
            jax.ShapeDtypeStruct((nb, KV_W, l_buf), F32),
        ],
        scratch_shapes=[
            pltpu.VMEM((D // LANES, nt, LANES), F32),
            pltpu.VMEM((D // LANES, nt, LANES), F32),
            pltpu.VMEM((nt, D), F32),
            pltpu.VMEM((sb, N_HEADS * s_new, KV_W), F32),
            pltpu.VMEM((sb, N_HEADS * s_new, l_buf + nt), F32),
            pltpu.VMEM((sb, N_HEADS * s_new, l_buf + nt), BF16),
            pltpu.VMEM((sb, N_HEADS * s_new, KV_W), F32),
            pltpu.VMEM((nt, D), F32),
        ],
        compiler_params=pltpu.CompilerParams(
            dimension_semantics=("arbitrary",), vmem_limit_bytes=VMEM_LIMIT),
        name="mix_sample",
    )(x2d, state_t, kt, vt, p["mix_pre_g"], p["w_in"], p["conv_dw_w"], p["conv_dw_b"], p["conv_ln_g"],
      p["conv_ln_b"], p["w_conv_out"], sink_col, p["w_attn_out"], p["w_out"], p["mix_post_g"], *tables)


def kernel(x_prompt, x_sample, state_conv, cache_k_win, cache_v_win, ffn1_pre_g, ffn1_w_up, ffn1_w_down, ffn1_post_g, mix_pre_g, w_in, conv_dw_w, conv_dw_b, conv_ln_g, conv_ln_b, w_conv_out, attn_sinks, w_attn_out, w_out, mix_post_g, ffn2_pre_g, ffn2_w_up, ffn2_w_down, ffn2_post_g):
    depth = w_in.shape[0]
    b, t, _ = x_prompt.shape
    nb, s_new, _ = x_sample.shape
    l_buf = cache_k_win.shape[2]
    mixer_weights = dict(w_in=w_in, w_conv_out=w_conv_out, w_attn_out=w_attn_out, w_out=w_out)
    ffn2_weights = dict(ffn2_w_up=ffn2_w_up, ffn2_w_down=ffn2_w_down)
    other = dict(ffn1_pre_g=ffn1_pre_g, ffn1_post_g=ffn1_post_g, mix_pre_g=mix_pre_g, conv_dw_w=conv_dw_w,
                 conv_dw_b=conv_dw_b, conv_ln_g=conv_ln_g, conv_ln_b=conv_ln_b, attn_sinks=attn_sinks,
                 mix_post_g=mix_post_g, ffn2_pre_g=ffn2_pre_g, ffn2_post_g=ffn2_post_g)
    tab_p = _rope_tables(jnp.arange(t))
    tab_s = _rope_tables(jnp.tile(PAST_LEN + jnp.arange(s_new), SAMPLE_SB))

    hp = x_prompt.reshape(b * t, D)
    hs = x_sample.reshape(nb * s_new, D)
    outs = [[] for _ in range(6)]
    for l in range(depth):
        p = {"ffn1_w_up": ffn1_w_up[l].astype(BF16), "ffn1_w_down": ffn1_w_down[l].astype(BF16)}
        for n, w in other.items():
            p[n] = w[l] if w[l].ndim == 2 or n == "attn_sinks" else w[l].reshape(1, -1)

        hp, converted = _ffn(hp, p["ffn1_pre_g"], p["ffn1_w_up"], p["ffn1_w_down"], p["ffn1_post_g"],
                             FFN_TM, tuple(w[l] for w in mixer_weights.values()))
        p.update(zip(mixer_weights, converted))
        hp, c1, k1, v1, converted = _mix_prompt(hp.reshape(b, t, D), p, tab_p,
                                                tuple(w[l] for w in ffn2_weights.values()))
        p.update(zip(ffn2_weights, converted))
        hp, _ = _ffn(hp.reshape(b * t, D), p["ffn2_pre_g"], p["ffn2_w_up"], p["ffn2_w_down"], p["ffn2_post_g"], FFN_TM)

        rows_last = lambda w: jnp.transpose(w, (0, 2, 3, 1)).reshape(nb, KV_W, l_buf)
        rows_back = lambda w: jnp.transpose(w.reshape(nb, N_KV, HEAD_DIM, l_buf), (0, 3, 1, 2))
        hs, _ = _ffn(hs, p["ffn1_pre_g"], p["ffn1_w_up"], p["ffn1_w_down"], p["ffn1_post_g"], FFN_TM_SMALL)
        hs, c2, k2, v2 = _mix_sample(hs, jnp.transpose(state_conv[l], (1, 0, 2)),
                                     rows_last(cache_k_win[l]), rows_last(cache_v_win[l]), p, tab_s)
        hs, _ = _ffn(hs, p["ffn2_pre_g"], p["ffn2_w_up"], p["ffn2_w_down"], p["ffn2_post_g"], FFN_TM_SMALL)

        kv_shape = lambda n: (n, -1, N_KV, HEAD_DIM)
        for acc, val in zip(outs, (c1, k1.reshape(kv_shape(b)), v1.reshape(kv_shape(b)),
                                   jnp.transpose(c2, (1, 0, 2)), rows_back(k2), rows_back(v2))):
            acc.append(val)
    return (hp.reshape(b, t, D), hs.reshape(nb, s_new, D), *(jnp.stack(o) for o in outs))
```

```python
import math

import jax
import jax.numpy as jnp
from jax import lax
from jax.experimental import pallas as pl
from jax.experimental.pallas import tpu as pltpu

D = 1024
N_HEADS = 16
HEAD_DIM = 64
N_KV = 4
GROUP = 4
ROT_DIM = 16
ROPE_THETA = 500000.0
WINDOW = 128
CONV_W = 31
HALO = CONV_W - 1
D_FF = 2816
KV_W = N_KV * HEAD_DIM
EPS = 1e-6
NEG_INF = -1e30
PAST_LEN = 8192
O_GLU_G = D
O_Q = 2 * D
O_K = O_Q + D
O_V = O_K + KV_W
O_GC = O_V + KV_W
O_GA = O_GC + D
IN_COLS = O_GA + D

LANES = 128
SUBLANES = 8
VMEM_LIMIT = 60 * 1024 * 1024

FFN_TM = 1024
FFN_TM_FIRST = 512
MIX_TQ = 512
CONV_STRIDE = 4
SAMPLE_SB = 16

F32 = jnp.float32
BF16 = jnp.bfloat16


def _dot(a, b):
    return jnp.dot(a, b, preferred_element_type=F32)


def _rmsnorm(x, g):
    r = lax.rsqrt(jnp.mean(x * x, axis=-1, keepdims=True) + EPS)
    return x * r * g


def _resident(shape):
    return pl.BlockSpec(shape, lambda *_: (0,) * len(shape), pipeline_mode=pl.Buffered(1))


def _ffn_kernel(x_ref, pre_ref, wup_ref, wdn_ref, post_ref, *rest):
    n_cast = len(rest) // 2
    cast_src, o_ref, cast_dst = rest[:n_cast], rest[n_cast], rest[n_cast + 1:]
    x = x_ref[...]
    h = _rmsnorm(x, pre_ref[...]).astype(BF16)
    gu = _dot(h, wup_ref[...])
    a = (jax.nn.silu(gu[:, 0:D_FF]) * gu[:, D_FF:]).astype(BF16)
    y = _dot(a, wdn_ref[...])
    o_ref[...] = x + 0.5 * _rmsnorm(y, post_ref[...])
    for src, dst in zip(cast_src, cast_dst):
        dst[...] = src[...].astype(BF16)


def _ffn(x2d, pre_g, w_up, w_down, post_g, tm, to_bf16=()):
    n = x2d.shape[0]
    steps = n // tm
    assert n % tm == 0 and D_FF % LANES == 0
    bf16_rows = 2 * SUBLANES
    tile = pl.BlockSpec((tm, D), lambda i: (i, 0))

    def slab(w):
        blocks = steps if w.shape[0] % (steps * bf16_rows) == 0 else steps // 2
        assert w.shape[0] % (blocks * bf16_rows) == 0
        return pl.BlockSpec((w.shape[0] // blocks, w.shape[1]), lambda i: (jnp.minimum(i, blocks - 1), 0))

    slabs = [slab(w) for w in to_bf16]
    y, *converted = pl.pallas_call(
        _ffn_kernel,
        grid=(steps,),
        in_specs=[
            tile,
            _resident((1, D)),
            _resident((D, 2 * D_FF)),
            _resident((D_FF, D)),
            _resident((1, D)),
            *slabs,
        ],
        out_specs=[tile, *slabs],
        out_shape=[jax.ShapeDtypeStruct((n, D), F32)] + [jax.ShapeDtypeStruct(w.shape, BF16) for w in to_bf16],
        compiler_params=pltpu.CompilerParams(
            dimension_semantics=("arbitrary",), vmem_limit_bytes=VMEM_LIMIT),
        name="ffn",
    )(x2d, pre_g, w_up, w_down, post_g, *to_bf16)
    return y, converted


def _rope(z, cos, sin_lo, sin_hi):
    half = ROT_DIM // 2
    outs = []
    for g in range(z.shape[-1] // LANES):
        zg = z[:, g * LANES:(g + 1) * LANES]
        outs.append(zg * cos
                    + pltpu.roll(zg, LANES - half, 1) * sin_lo
                    + pltpu.roll(zg, half, 1) * sin_hi)
    return outs


def _rope_tables(pos):
    half = ROT_DIM // 2
    inv = jnp.exp(-math.log(ROPE_THETA) * jnp.arange(0, ROT_DIM, 2, dtype=F32) / ROT_DIM)
    ang = pos.astype(F32)[:, None] * inv[None, :]
    cos, sin = jnp.cos(ang), jnp.sin(ang)
    n = pos.shape[0]
    pad = jnp.zeros((n, HEAD_DIM - ROT_DIM), F32)
    zero = jnp.zeros((n, half), F32)
    cos_h = jnp.concatenate([cos, cos, pad + 1.0], axis=1)
    lo_h = jnp.concatenate([-sin, zero, pad], axis=1)
    hi_h = jnp.concatenate([zero, sin, pad], axis=1)
    rep = LANES // HEAD_DIM
    return tuple(jnp.tile(t, (1, rep)) for t in (cos_h, lo_h, hi_h))


def _ln_silu(c, g, b):
    mu = jnp.mean(c, axis=-1, keepdims=True)
    var = jnp.mean(jnp.square(c - mu), axis=-1, keepdims=True)
    return jax.nn.silu((c - mu) * lax.rsqrt(var + EPS) * g + b)


def _dup_halves(z, kv):
    src = z[:, (kv // 2) * LANES:(kv // 2 + 1) * LANES]
    other = pltpu.roll(src, HEAD_DIM, 1)
    low = lax.broadcasted_iota(jnp.int32, src.shape, 1) < HEAD_DIM
    return jnp.where(low, src, other) if kv % 2 == 0 else jnp.where(low, other, src)


def _mix_prompt_kernel(x_ref, pre_ref, win_ref, cw_ref, cb_ref, lng_ref, lnb_ref, wco_ref, sinks_ref,
                       wao_ref, wo_ref, post_ref, cos_ref, slo_ref, shi_ref,
                       y_ref, cst_ref, kw_ref, vw_ref,
                       ubuf, cbuf, qbuf, kdup, vdup, gbuf, obuf):
    tq = x_ref.shape[0]
    t = pl.program_id(1)
    n_slab = D // LANES

    @pl.when(t == 0)
    def _():
        ubuf[:, 0:32, :] = jnp.zeros((n_slab, 32, LANES), F32)
        kdup[:, 0:WINDOW, :] = jnp.zeros((N_KV, WINDOW, LANES), BF16)
        vdup[:, 0:WINDOW, :] = jnp.zeros((N_KV, WINDOW, LANES), BF16)

    @pl.when(t > 0)
    def _():
        ubuf[:, 0:32, :] = ubuf[:, tq:tq + 32, :]
        kdup[:, 0:WINDOW, :] = kdup[:, tq:tq + WINDOW, :]
        vdup[:, 0:WINDOW, :] = vdup[:, tq:tq + WINDOW, :]

    x = x_ref[...]
    h = _rmsnorm(x, pre_ref[...]).astype(BF16)

    u = _dot(h, win_ref[:, 0:D]) * jax.nn.sigmoid(_dot(h, win_ref[:, O_GLU_G:O_GLU_G + D]))
    for lg in range(n_slab):
        ubuf[lg, 32:32 + tq, :] = u[:, lg * LANES:(lg + 1) * LANES]
    cst_ref[...] = u[tq - HALO:, :]

    rows_per_unit = CONV_STRIDE * SUBLANES
    for lg in range(n_slab):
        lanes = slice(lg * LANES, (lg + 1) * LANES)
        for base in range(0, tq, rows_per_unit):
            w = {}
            acc = [jnp.broadcast_to(cb_ref[:, lanes], (SUBLANES, LANES))] * CONV_STRIDE
            for off in range(CONV_STRIDE - 1 + CONV_W):
                win = ubuf[lg, pl.ds(base + off + 2, SUBLANES, stride=CONV_STRIDE), :]
                if off < CONV_W:
                    w[off] = jnp.broadcast_to(cw_ref[off:off + 1, lanes], (SUBLANES, LANES))
                for r in range(CONV_STRIDE):
                    if 0 <= off - r < CONV_W:
                        acc[r] = acc[r] + w[off - r] * win
            for r in range(CONV_STRIDE):
                cbuf[lg, pl.ds(base + r, SUBLANES, stride=CONV_STRIDE), :] = acc[r]

    cos, slo, shi = cos_ref[...], slo_ref[...], shi_ref[...]
    chunk = 4 * LANES

    def project(col):
        return _dot(h, win_ref[:, col:col + chunk])

    def kv_chunk(z):
        rk = jnp.concatenate(_rope(z[:, 0:KV_W], cos, slo, shi), axis=-1)
        zv = z[:, KV_W:2 * KV_W]
        for kv in range(N_KV):
            kdup[kv, WINDOW:WINDOW + tq, :] = _dup_halves(rk, kv).astype(BF16)
            vdup[kv, WINDOW:WINDOW + tq, :] = _dup_halves(zv, kv).astype(BF16)
        kw_ref[...] = rk[tq - WINDOW:, :]
        vw_ref[...] = zv[tq - WINDOW:, :]

    def q_chunk(c, z):
        for g, rq in enumerate(_rope(z, cos, slo, shi)):
            col = c * chunk + g * LANES
            qbuf[:, col:col + LANES] = (rq * (HEAD_DIM ** -0.5)).astype(BF16)

    def gate_chunk(c, z):
        gbuf[:, c * chunk:(c + 1) * chunk] = jax.nn.sigmoid(z)

    kv_chunk(project(O_K))
    for c in range(D // chunk):
        q_chunk(c, project(O_Q + c * chunk))
    for c in range(2 * D // chunk):
        gate_chunk(c, project(O_GC + c * chunk))

    c = jnp.concatenate([cbuf[lg] for lg in range(n_slab)], axis=-1)
    cs = _ln_silu(c, lng_ref[...], lnb_ref[...]).astype(BF16)
    conv_out = _dot(cs, wco_ref[...])

    row = lax.broadcasted_iota(jnp.int32, (WINDOW, 2 * WINDOW), 0)
    col = lax.broadcasted_iota(jnp.int32, (WINDOW, 2 * WINDOW), 1)
    band = (col > row) & (col <= row + WINDOW)
    low_q = lax.broadcasted_iota(jnp.int32, (WINDOW, LANES), 1) < HEAD_DIM
    low_kv = lax.broadcasted_iota(jnp.int32, (2 * WINDOW, LANES), 1) < HEAD_DIM
    zero_q = jnp.zeros((WINDOW, LANES), BF16)
    zero_kv = jnp.zeros((2 * WINDOW, LANES), BF16)
    sink_slot = lax.broadcasted_iota(jnp.int32, (WINDOW, LANES), 1) == 0
    not_key0 = lax.broadcasted_iota(jnp.int32, (2 * WINDOW, LANES), 0) > 0
    no_key = jnp.full((WINDOW, LANES), NEG_INF, F32)
    ones_bd = jnp.concatenate([low_kv, ~low_kv], axis=0).astype(F32).astype(BF16)
    for i in range(tq // WINDOW):
        r0 = i * WINDOW
        mask = band if i > 0 else band & (col >= jnp.where(t > 0, 0, WINDOW))
        for kv in range(N_KV):
            kd = kdup[kv, r0:r0 + 2 * WINDOW, :]
            vd = vdup[kv, r0:r0 + 2 * WINDOW, :]
            v_bd = jnp.concatenate([jnp.where(low_kv & not_key0, vd, zero_kv),
                                    jnp.where(~low_kv & not_key0, vd, zero_kv)], axis=0)
            v_aug = jnp.concatenate([v_bd, ones_bd], axis=-1)
            q_rows = []
            for pair in range(GROUP // 2):
                tile_col = (kv * (GROUP // 2) + pair) * LANES
                qp = qbuf[r0:r0 + WINDOW, tile_col:tile_col + LANES]
                q_rows += [jnp.where(low_q, qp, zero_q), jnp.where(low_q, zero_q, qp)]
            s_all = lax.dot_general(jnp.concatenate(q_rows, axis=0), kd, (((1,), (1,)), ((), ())),
                                    preferred_element_type=F32)
            probs = []
            for g in range(GROUP):
                fill = jnp.concatenate([jnp.where(sink_slot, sinks_ref[kv * GROUP + g], no_key), no_key], axis=-1)
                s = jnp.where(mask, s_all[g * WINDOW:(g + 1) * WINDOW], fill)
                probs.append(jnp.exp(s - jnp.max(s, axis=-1, keepdims=True)).astype(BF16))
            for pair in range(GROUP // 2):
                tile_col = (kv * (GROUP // 2) + pair) * LANES
                oa = _dot(jnp.concatenate(probs[2 * pair:2 * pair + 2], axis=-1), v_aug)
                obuf[r0:r0 + WINDOW, tile_col:tile_col + LANES] = (oa[:, 0:LANES] / oa[:, LANES:]).astype(BF16)
    attn_out = _dot(obuf[...], wao_ref[...])

    merged = gbuf[:, 0:D] * conv_out + gbuf[:, D:2 * D] * attn_out
    y = _dot(merged.astype(BF16), wo_ref[...])
    y_ref[...] = x + _rmsnorm(y, post_ref[...])


def _mix_prompt(x, p, tables):
    b, t, _ = x.shape
    tq = MIX_TQ
    assert t % tq == 0 and tq % WINDOW == 0 and tq % (CONV_STRIDE * SUBLANES) == 0
    tile = lambda bi, ti: (bi, ti, 0)
    per_seq = lambda bi, ti: (bi, 0, 0)
    tab = pl.BlockSpec((tq, LANES), lambda bi, ti: (ti, 0))
    return pl.pallas_call(
        _mix_prompt_kernel,
        grid=(b, t // tq),
        in_specs=[
            pl.BlockSpec((None, tq, D), tile),
            _resident((1, D)),
            _resident((D, IN_COLS)),
            _resident((CONV_W, D)),
            _resident((1, D)),
            _resident((1, D)),
            _resident((1, D)),
            _resident((D, D)),
            pl.BlockSpec(memory_space=pltpu.SMEM),
            _resident((D, D)),
            _resident((D, D)),
            _resident((1, D)),
            tab, tab, tab,
        ],
        out_specs=[
            pl.BlockSpec((None, tq, D), tile),
            pl.BlockSpec((None, HALO, D), per_seq),
            pl.BlockSpec((None, WINDOW, KV_W), per_seq),
            pl.BlockSpec((None, WINDOW, KV_W), per_seq),
        ],
        out_shape=[
            jax.ShapeDtypeStruct((b, t, D), F32),
            jax.ShapeDtypeStruct((b, HALO, D), F32),
            jax.ShapeDtypeStruct((b, WINDOW, KV_W), F32),
            jax.ShapeDtypeStruct((b, WINDOW, KV_W), F32),
        ],
        scratch_shapes=[
            pltpu.VMEM((D // LANES, 32 + tq, LANES), F32),
            pltpu.VMEM((D // LANES, tq, LANES), F32),
            pltpu.VMEM((tq, D), BF16),
            pltpu.VMEM((N_KV, WINDOW + tq, LANES), BF16),
            pltpu.VMEM((N_KV, WINDOW + tq, LANES), BF16),
            pltpu.VMEM((tq, 2 * D), F32),
            pltpu.VMEM((tq, D), BF16),
        ],
        compiler_params=pltpu.CompilerParams(
            dimension_semantics=("arbitrary", "arbitrary"), vmem_limit_bytes=VMEM_LIMIT),
        name="mix_prompt",
    )(x, p["mix_pre_g"], p["w_in"], p["conv_dw_w"], p["conv_dw_b"], p["conv_ln_g"], p["conv_ln_b"],
      p["w_conv_out"], p["attn_sinks"], p["w_attn_out"], p["w_out"], p["mix_post_g"], *tables)


def _mix_sample_kernel(x_ref, st_ref, kt_ref, vt_ref, pre_ref, win_ref, cw_ref, cb_ref, lng_ref, lnb_ref,
                       wco_ref, sinkcol_ref, wao_ref, wo_ref, post_ref, cos_ref, slo_ref, shi_ref,
                       y_ref, cst_ref, kto_ref, vto_ref,
                       ubuf, cbuf, qf, qm, sbuf, pbuf, obuf, of):
    nt = x_ref.shape[0]
    s_new = SUBLANES
    sb = nt // s_new
    l_buf = kt_ref.shape[2]
    n_slab = D // LANES

    x = x_ref[...]
    h = _rmsnorm(x, pre_ref[...]).astype(BF16)

    u = _dot(h, win_ref[:, 0:D]) * jax.nn.sigmoid(_dot(h, win_ref[:, O_GLU_G:O_GLU_G + D]))
    for lg in range(n_slab):
        ubuf[lg] = u[:, lg * LANES:(lg + 1) * LANES]

    def conv_row(i, lg):
        if i < HALO:
            return st_ref[i, :, lg * LANES:(lg + 1) * LANES]
        return ubuf[lg, pl.ds(i - HALO, sb, stride=s_new), :]

    for lg in range(n_slab):
        lanes = slice(lg * LANES, (lg + 1) * LANES)
        w = [jnp.broadcast_to(cw_ref[j:j + 1, lanes], (sb, LANES)) for j in range(CONV_W)]
        acc = [jnp.broadcast_to(cb_ref[:, lanes], (sb, LANES))] * s_new
        for i in range(HALO + s_new):
            row_i = conv_row(i, lg)
            for tok in range(s_new):
                if 0 <= i - tok < CONV_W:
                    acc[tok] = acc[tok] + w[i - tok] * row_i
            if i >= s_new:
                cst_ref[i - s_new, :, lanes] = row_i
        for tok in range(s_new):
            cbuf[lg, pl.ds(tok, sb, stride=s_new), :] = acc[tok]
    c = jnp.concatenate([cbuf[lg] for lg in range(n_slab)], axis=-1)
    cs = _ln_silu(c, lng_ref[...], lnb_ref[...]).astype(BF16)
    conv_out = _dot(cs, wco_ref[...])

    cos, slo, shi = cos_ref[...], slo_ref[...], shi_ref[...]
    zq = _dot(h, win_ref[:, O_Q:O_Q + D])
    for g, rq in enumerate(_rope(zq, cos, slo, shi)):
        qf[:, g * LANES:(g + 1) * LANES] = rq * (HEAD_DIM ** -0.5)
    rk = jnp.concatenate(_rope(_dot(h, win_ref[:, O_K:O_K + KV_W]), cos, slo, shi), axis=-1)
    zv = _dot(h, win_ref[:, O_V:O_V + KV_W])
    k_new, v_new = rk.astype(BF16), zv.astype(BF16)
    k_new_t, v_new_t = rk.T, zv.T

    n_rows = N_HEADS * s_new
    nt_dims = (((1,), (1,)), ((), ()))
    low_half = lax.broadcasted_iota(jnp.int32, (nt, LANES), 1) < HEAD_DIM
    zeros = jnp.zeros((sb, s_new, LANES), F32)

    for head in range(N_HEADS):
        kv = head // GROUP
        src = qf[:, (head // 2) * LANES:(head // 2 + 1) * LANES]
        if head % 2 != kv % 2:
            src = pltpu.roll(src, HEAD_DIM, 1)
        piece = jnp.where(low_half if kv % 2 == 0 else ~low_half, src, 0.0).reshape(sb, s_new, LANES)
        qm[:, head * s_new:(head + 1) * s_new, :] = jnp.concatenate(
            [piece, zeros] if kv // 2 == 0 else [zeros, piece], axis=-1)

    keep_old = lax.broadcasted_iota(jnp.int32, (KV_W, LANES), 1) < l_buf - s_new
    shift = l_buf - s_new

    def new_columns(z_t, b):
        amount = (shift - b * s_new) % LANES
        return pltpu.roll(z_t, amount, 1) if amount else z_t

    for b in range(sb):
        qb = qm[b].astype(BF16)
        kt = kt_ref[b]
        sbuf[b] = jnp.concatenate([_dot(qb, kt.astype(BF16)),
                                   lax.dot_general(qb, k_new, nt_dims, preferred_element_type=F32)], axis=-1)
        kto_ref[b] = jnp.where(keep_old, pltpu.roll(kt, shift, 1), new_columns(k_new_t, b))

    shape3 = (sb, n_rows, l_buf + nt)
    seq = lax.broadcasted_iota(jnp.int32, shape3, 0)
    tok = jnp.bitwise_and(lax.broadcasted_iota(jnp.int32, shape3, 1), s_new - 1)
    col = lax.broadcasted_iota(jnp.int32, shape3, 2)
    first_new = l_buf + seq * s_new
    mask = ((col < l_buf) & (col > tok + l_buf - WINDOW)) | ((col >= first_new) & (col <= first_new + tok))
    fill = jnp.where(col == 0, sinkcol_ref[...].reshape(1, n_rows, 1), NEG_INF)
    s = jnp.where(mask, sbuf[...], fill)
    pbuf[...] = jnp.exp(s - jnp.max(s, axis=-1, keepdims=True)).astype(BF16)

    ones = jnp.ones((LANES, LANES), BF16)
    not_key0 = lax.broadcasted_iota(jnp.int32, (KV_W, l_buf), 1) > 0
    v_new_aug = jnp.concatenate([v_new, ones], axis=1)
    for b in range(sb):
        vt = vt_ref[b]
        pb = pbuf[b]
        vt_aug = jnp.concatenate([jnp.where(not_key0, vt, 0.0).astype(BF16), ones], axis=0)
        oa = (lax.dot_general(pb[:, 0:l_buf], vt_aug, nt_dims, preferred_element_type=F32)
              + _dot(pb[:, l_buf:], v_new_aug))
        den = oa[:, KV_W:]
        obuf[b] = oa[:, 0:KV_W] / jnp.concatenate([den] * (KV_W // LANES), axis=-1)
        vto_ref[b] = jnp.where(keep_old, pltpu.roll(vt, shift, 1), new_columns(v_new_t, b))

    for pair in range(N_HEADS // 2):
        kv = (2 * pair) // GROUP
        halves = []
        for head in (2 * pair, 2 * pair + 1):
            blk = obuf[:, head * s_new:(head + 1) * s_new, (kv // 2) * LANES:(kv // 2 + 1) * LANES]
            blk = blk.reshape(nt, LANES)
            halves.append(pltpu.roll(blk, HEAD_DIM, 1) if head % 2 != kv % 2 else blk)
        of[:, pair * LANES:(pair + 1) * LANES] = jnp.where(low_half, halves[0], halves[1])
    attn_out = _dot(of[...].astype(BF16), wao_ref[...])

    g_conv = jax.nn.sigmoid(_dot(h, win_ref[:, O_GC:O_GC + D]))
    g_attn = jax.nn.sigmoid(_dot(h, win_ref[:, O_GA:O_GA + D]))
    y = _dot((g_conv * conv_out + g_attn * attn_out).astype(BF16), wo_ref[...])
    y_ref[...] = x + _rmsnorm(y, post_ref[...])


def _mix_sample(x2d, state_t, kt, vt, p, tables):
    nb, _, l_buf = kt.shape
    s_new = x2d.shape[0] // nb
    sb = SAMPLE_SB
    nt = sb * s_new
    assert s_new == SUBLANES and l_buf == LANES == WINDOW and nt == LANES and nb % sb == 0
    sink_col = jnp.repeat(p["attn_sinks"], s_new).reshape(N_HEADS * s_new, 1)
    rows = lambda i: (i, 0)
    seqs = lambda i: (i, 0, 0)
    taps = lambda i: (0, i, 0)
    tab = pl.BlockSpec((nt, LANES), lambda i: (0, 0), pipeline_mode=pl.Buffered(1))
    return pl.pallas_call(
        _mix_sample_kernel,
        grid=(nb // sb,),
        in_specs=[
            pl.BlockSpec((nt, D), rows),
            pl.BlockSpec((HALO, sb, D), taps),
            pl.BlockSpec((sb, KV_W, l_buf), seqs),
            pl.BlockSpec((sb, KV_W, l_buf), seqs),
            _resident((1, D)),
            _resident((D, IN_COLS)),
            _resident((CONV_W, D)),
            _resident((1, D)),
            _resident((1, D)),
            _resident((1, D)),
            _resident((D, D)),
            _resident((N_HEADS * s_new, 1)),
            _resident((D, D)),
            _resident((D, D)),
            _resident((1, D)),
            tab, tab, tab,
        ],
        out_specs=[
            pl.BlockSpec((nt, D), rows),
            pl.BlockSpec((HALO, sb, D), taps),
            pl.BlockSpec((sb, KV_W, l_buf), seqs),
            pl.BlockSpec((sb, KV_W, l_buf), seqs),
        ],
        out_shape=[
            jax.ShapeDtypeStruct(x2d.shape, F32),
            jax.ShapeDtypeStruct((HALO, nb, D), F32),
            jax.ShapeDtypeStruct((nb, KV_W, l_buf), F32),
            jax.ShapeDtypeStruct((nb, KV_W, l_buf), F32),
        ],
        scratch_shapes=[
            pltpu.VMEM((D // LANES, nt, LANES), F32),
            pltpu.VMEM((D // LANES, nt, LANES), F32),
            pltpu.VMEM((nt, D), F32),
            pltpu.VMEM((sb, N_HEADS * s_new, KV_W), F32),
            pltpu.VMEM((sb, N_HEADS * s_new, l_buf + nt), F32),
            pltpu.VMEM((sb, N_HEADS * s_new, l_buf + nt), BF16),
            pltpu.VMEM((sb, N_HEADS * s_new, KV_W), F32),
            pltpu.VMEM((nt, D), F32),
        ],
        compiler_params=pltpu.CompilerParams(
            dimension_semantics=("arbitrary",), vmem_limit_bytes=VMEM_LIMIT),
        name="mix_sample",
    )(x2d, state_t, kt, vt, p["mix_pre_g"], p["w_in"], p["conv_dw_w"], p["conv_dw_b"], p["conv_ln_g"],
      p["conv_ln_b"], p["w_conv_out"], sink_col, p["w_attn_out"], p["w_out"], p["mix_post_g"], *tables)


def kernel(x_prompt, x_sample, state_conv, cache_k_win, cache_v_win, ffn1_pre_g, ffn1_w_up, ffn1_w_down, ffn1_post_g, mix_pre_g, w_in, conv_dw_w, conv_dw_b, conv_ln_g, conv_ln_b, w_conv_out, attn_sinks, w_attn_out, w_out, mix_post_g, ffn2_pre_g, ffn2_w_up, ffn2_w_down, ffn2_post_g):
    depth = w_in.shape[0]
    b, t, _ = x_prompt.shape
    nb, s_new, _ = x_sample.shape
    l_buf = cache_k_win.shape[2]
    later_weights = dict(w_in=w_in, w_conv_out=w_conv_out, w_attn_out=w_attn_out, w_out=w_out,
                         ffn2_w_up=ffn2_w_up, ffn2_w_down=ffn2_w_down)
    other = dict(ffn1_pre_g=ffn1_pre_g, ffn1_post_g=ffn1_post_g, mix_pre_g=mix_pre_g, conv_dw_w=conv_dw_w,
                 conv_dw_b=conv_dw_b, conv_ln_g=conv_ln_g, conv_ln_b=conv_ln_b, attn_sinks=attn_sinks,
                 mix_post_g=mix_post_g, ffn2_pre_g=ffn2_pre_g, ffn2_post_g=ffn2_post_g)
    tab_p = _rope_tables(jnp.arange(t))
    tab_s = _rope_tables(jnp.tile(PAST_LEN + jnp.arange(s_new), SAMPLE_SB))

    hp = x_prompt.reshape(b * t, D)
    hs = x_sample.reshape(nb * s_new, D)
    outs = [[] for _ in range(6)]
    for l in range(depth):
        p = {"ffn1_w_up": ffn1_w_up[l].astype(BF16), "ffn1_w_down": ffn1_w_down[l].astype(BF16)}
        for n, w in other.items():
            p[n] = w[l] if w[l].ndim == 2 or n == "attn_sinks" else w[l].reshape(1, -1)

        hp, converted = _ffn(hp, p["ffn1_pre_g"], p["ffn1_w_up"], p["ffn1_w_down"], p["ffn1_post_g"],
                             FFN_TM, tuple(w[l] for w in later_weights.values()))
        p.update(zip(later_weights, converted))
        hp, c1, k1, v1 = _mix_prompt(hp.reshape(b, t, D), p, tab_p)
        hp, _ = _ffn(hp.reshape(b * t, D), p["ffn2_pre_g"], p["ffn2_w_up"], p["ffn2_w_down"], p["ffn2_post_g"], FFN_TM)

        rows_last = lambda w: jnp.transpose(w, (0, 2, 3, 1)).reshape(nb, KV_W, l_buf)
        rows_back = lambda w: jnp.transpose(w.reshape(nb, N_KV, HEAD_DIM, l_buf), (0, 3, 1, 2))
        hs, _ = _ffn(hs, p["ffn1_pre_g"], p["ffn1_w_up"], p["ffn1_w_down"], p["ffn1_post_g"], FFN_TM_FIRST)
        hs, c2, k2, v2 = _mix_sample(hs, jnp.transpose(state_conv[l], (1, 0, 2)),
                                     rows_last(cache_k_win[l]), rows_last(cache_v_win[l]), p, tab_s)
        hs, _ = _ffn(hs, p["ffn2_pre_g"], p["ffn2_w_up"], p["ffn2_w_down"], p["ffn2_post_g"], FFN_TM_FIRST)

        kv_shape = lambda n: (n, -1, N_KV, HEAD_DIM)
        for acc, val in zip(outs, (c1, k1.reshape(kv_shape(b)), v1.reshape(kv_shape(b)),
                                   jnp.transpose(c2, (1, 0, 2)), rows_back(k2), rows_back(v2))):
            acc.append(val)
    return (hp.reshape(b, t, D), hs.reshape(nb, s_new, D), *(jnp.stack(o) for o in outs))
```
